```python
import math
import jax, jax.numpy as jnp
from jax import lax
import numpy as np

D_MODEL = 1024
BATCH = 16
SEQ = 2048
DEPTH = 1

GLA_HEADS = 4
GLA_DK = 64
GLA_DV = 128
GLA_QK_WIDTH = GLA_HEADS * GLA_DK
GLA_WIDTH = GLA_HEADS * GLA_DV
GATE_RANK = 16
GATE_NORMALIZER = 16.0
CHUNK = 64
CONV_WIDTH = D_MODEL // 2
CONV_GROUPS = 8
CONV_K = 3
MIX_WIDTH = GLA_WIDTH + CONV_WIDTH
D_FF = int(math.ceil(8 * D_MODEL / 3 / 256) * 256)
NORM_EPS = 1e-6

COL_Q = GLA_QK_WIDTH
COL_K = GLA_QK_WIDTH
COL_V = GLA_WIDTH
COL_G = GLA_WIDTH
COL_A = GATE_RANK
COL_CB = CONV_WIDTH
COL_CC = CONV_WIDTH
COL_CH = CONV_WIDTH
IN_COLS = COL_Q + COL_K + COL_V + COL_G + COL_A + COL_CB + COL_CC + COL_CH

kernel_name = "hymba_gla_shortconv_block"


def rmsnorm(x, g):
    xf = x.astype(jnp.float32)
    y = xf * lax.rsqrt(jnp.mean(xf * xf, axis=-1, keepdims=True) + NORM_EPS)
    return (y * g.astype(jnp.float32)).astype(x.dtype)


def gla_chunked(q, k, v, log_a):
    bsz, seq, heads, dk = q.shape
    dv = v.shape[-1]
    n_chunks = seq // CHUNK

    def to_chunks(t):
        return t.astype(jnp.float32).reshape(bsz, n_chunks, CHUNK, heads, t.shape[-1]).transpose(1, 0, 3, 2, 4)

    q, k, v, log_a = to_chunks(q), to_chunks(k), to_chunks(v), to_chunks(log_a)
    b = jnp.cumsum(log_a, axis=-2)
    b_last = b[..., -1:, :]
    q_i = q * jnp.exp(b)
    k_i = k * jnp.exp(-b)
    k_s = k * jnp.exp(b_last - b)

    causal = jnp.tril(jnp.ones((CHUNK, CHUNK), dtype=bool))
    scores = jnp.einsum('nbhck,nbhsk->nbhcs', q_i, k_i)
    scores = jnp.where(causal, scores, 0.0)
    o_intra = jnp.einsum('nbhcs,nbhsv->nbhcv', scores, v)

    chunk_kv = jnp.einsum('nbhsk,nbhsv->nbhkv', k_s, v)
    chunk_decay = jnp.exp(b_last[..., 0, :])

    def step(state, inp):
        dec, kv_c = inp
        return dec[..., None] * state + kv_c, state

    state0 = jnp.zeros((bsz, heads, dk, dv), jnp.float32)
    _, state_prev = lax.scan(step, state0, (chunk_decay, chunk_kv))
    o_inter = jnp.einsum('nbhck,nbhkv->nbhcv', q_i, state_prev)

    o = o_intra + o_inter
    return o.transpose(1, 0, 3, 2, 4).reshape(bsz, seq, heads, dv)


def causal_depthwise_conv(u, w):
    seq = u.shape[1]
    u_pad = jnp.pad(u, ((0, 0), (CONV_K - 1, 0), (0, 0)))
    y = w[0] * u_pad[:, 0:seq]
    for j in range(1, CONV_K):
        y = y + w[j] * u_pad[:, j:j + seq]
    return y


def setup_inputs(seed: int = 0) -> dict:
    key = jax.random.key(seed)
    ks = jax.random.split(key, 16)
    f32 = jnp.float32
    n = lambda k, shape, scale: (jax.random.normal(k, shape, f32) * scale)
    return {
        "x": n(ks[0], (BATCH, SEQ, D_MODEL), 1.0),
        "norm1_g": 1.0 + n(ks[1], (DEPTH, D_MODEL), 0.02),
        "w_in": n(ks[2], (DEPTH, D_MODEL, IN_COLS), D_MODEL ** -0.5),
        "w_gate_up": n(ks[3], (DEPTH, GATE_RANK, GLA_QK_WIDTH), GATE_RANK ** -0.5),
        "b_gate": n(ks[4], (DEPTH, GLA_QK_WIDTH), 0.01),
        "gla_norm_g": 1.0 + n(ks[5], (DEPTH, GLA_DV), 0.02),
        "conv_w": n(ks[6], (DEPTH, CONV_K, CONV_WIDTH), CONV_K ** -0.5),
        "w_out": n(ks[7], (DEPTH, MIX_WIDTH, D_MODEL), MIX_WIDTH ** -0.5),
        "norm2_g": 1.0 + n(ks[8], (DEPTH, D_MODEL), 0.02),
        "w_ffn_gate": n(ks[9], (DEPTH, D_MODEL, D_FF), D_MODEL ** -0.5),
        "w_ffn_up": n(ks[10], (DEPTH, D_MODEL, D_FF), D_MODEL ** -0.5),
        "w_ffn_down": n(ks[11], (DEPTH, D_FF, D_MODEL), D_FF ** -0.5),
        "norm_f_g": 1.0 + n(ks[12], (D_MODEL,), 0.02),
    }


def reference(x, norm1_g, w_in, w_gate_up, b_gate, gla_norm_g, conv_w, w_out,
              norm2_g, w_ffn_gate, w_ffn_up, w_ffn_down, norm_f_g):
    bsz, seq, _ = x.shape
    splits = np.cumsum([COL_Q, COL_K, COL_V, COL_G, COL_A, COL_CB, COL_CC])
    for l in range(DEPTH):
        h = rmsnorm(x, norm1_g[l])
        proj = h @ w_in[l]
        p_q, p_k, p_v, p_g, p_a, c_b, c_c, c_h = jnp.split(proj, splits, axis=-1)

        q = p_q.reshape(bsz, seq, GLA_HEADS, GLA_DK) * (GLA_DK ** -0.5)
        k = p_k.reshape(bsz, seq, GLA_HEADS, GLA_DK)
        v = p_v.reshape(bsz, seq, GLA_HEADS, GLA_DV)
        gate_logits = (p_a @ w_gate_up[l] + b_gate[l]).astype(jnp.float32)
        log_a = (jax.nn.log_sigmoid(gate_logits) / GATE_NORMALIZER).reshape(bsz, seq, GLA_HEADS, GLA_DK)
        o = gla_chunked(q, k, v, log_a)
        o = rmsnorm(o, gla_norm_g[l]).astype(x.dtype)
        o = o * jax.nn.silu(p_g.reshape(bsz, seq, GLA_HEADS, GLA_DV))
        o_gla = o.reshape(bsz, seq, GLA_WIDTH)

        u = c_c * c_h
        o_conv = c_b * causal_depthwise_conv(u, conv_w[l])

        mixed = jnp.concatenate([o_gla, o_conv], axis=-1) @ w_out[l]
        x = x + mixed

        h2 = rmsnorm(x, norm2_g[l])
        ffn = (jax.nn.silu(h2 @ w_ffn_gate[l]) * (h2 @ w_ffn_up[l])) @ w_ffn_down[l]
        x = x + ffn
    return rmsnorm(x, norm_f_g)
```

```python
import functools

import jax
import jax.numpy as jnp
from jax import lax
from jax.experimental import pallas as pl
from jax.experimental.pallas import tpu as pltpu

GLA_HEADS = 4
GLA_DK = 64
GLA_DV = 128
GATE_RANK = 16
GATE_NORMALIZER = 16.0
CHUNK = 64
CONV_K = 3
NORM_EPS = 1e-6

LANES = 128
SUBLANES = 8
QK_W = GLA_HEADS * GLA_DK
V_W = GLA_HEADS * GLA_DV

MIXER_ROWS = 512
FFN_ROWS = 512
VMEM_LIMIT_BYTES = 56 * 1024 * 1024

_NT = (((1,), (1,)), ((), ()))
_TN = (((0,), (0,)), ((), ()))


def _rms(x, g):
    y = x * lax.rsqrt(jnp.mean(x * x, axis=-1, keepdims=True) + NORM_EPS)
    return y * g


def _dot(a, b):
    return jnp.dot(a, b, preferred_element_type=jnp.float32)


def _sigmoid(x):
    return 1.0 / (1.0 + jnp.exp(-x))


def _mixer_kernel(x_ref, g1_ref, w_ref, wup_ref, bg_ref, gn_ref, cw_ref, wout_ref,
                  out_ref, h_s, q_s, k_s, la_s, v_s, o_s, u_s, st_s, mix_s,
                  *, rows, conv_w):
    f32, bf16 = jnp.float32, jnp.bfloat16
    c_q, c_v, c_g = 0, 2 * QK_W, 2 * QK_W + V_W
    c_cb = c_g + V_W
    c_cc, c_ch = c_cb + conv_w, c_cb + 2 * conv_w
    c_a = c_cb + 3 * conv_w

    @pl.when(pl.program_id(1) == 0)
    def _():
        st_s[...] = jnp.zeros_like(st_s)
        u_s[0:SUBLANES, :] = jnp.zeros((SUBLANES, conv_w), f32)

    x = x_ref[0]
    h_s[...] = _rms(x, g1_ref[...]).astype(bf16)
    hb = h_s[...]

    qk = _dot(hb, w_ref[:, c_q:c_v])
    q_s[...] = qk[:, :QK_W] * (GLA_DK ** -0.5)
    k_s[...] = qk[:, QK_W:]
    v_s[...] = _dot(hb, w_ref[:, c_v:c_g]).astype(bf16)
    pa = _dot(hb, w_ref[:, c_a:c_a + LANES])
    gl = _dot(pa.astype(bf16), wup_ref[...]) + bg_ref[...]
    la_s[...] = (jnp.minimum(gl, 0.0) - jnp.log1p(jnp.exp(-jnp.abs(gl)))) * (1.0 / GATE_NORMALIZER)

    row = lax.broadcasted_iota(jnp.int32, (CHUNK, CHUNK), 0)
    col = lax.broadcasted_iota(jnp.int32, (CHUNK, CHUNK), 1)
    causal = col <= row
    tri = causal.astype(bf16)
    lane = lax.broadcasted_iota(jnp.int32, (CHUNK, LANES), 1)
    first_half = lane < GLA_DK

    def chunk_body(c, carry):
        r0 = pl.multiple_of(c * CHUNK, CHUNK)
        rs = pl.ds(r0, CHUNK)
        la = la_s[rs, :]
        la_hi = la.astype(bf16)
        la_lo = (la - la_hi.astype(f32)).astype(bf16)
        b = _dot(tri, la_hi) + _dot(tri, la_lo)
        b_last = b[CHUNK - 1:CHUNK, :]
        q_i = q_s[rs, :] * jnp.exp(b)
        k = k_s[rs, :]
        k_i = (k * jnp.exp(-b)).astype(bf16)
        k_e = (k * jnp.exp(b_last - b)).astype(bf16)
        decay = jnp.exp(b_last)
        for p in range(GLA_HEADS // 2):
            ls = slice(p * LANES, (p + 1) * LANES)
            q_p, k_ip, k_ep, dec_p = q_i[:, ls], k_i[:, ls], k_e[:, ls], decay[:, ls]
            for hh in range(2):
                hd = 2 * p + hh
                vs = slice(hd * GLA_DV, (hd + 1) * GLA_DV)
                q_m = jnp.where(first_half if hh == 0 else ~first_half, q_p, 0.0).astype(bf16)
                sc = lax.dot_general(q_m, k_ip, _NT, preferred_element_type=f32)
                sc = jnp.where(causal, sc, 0.0).astype(bf16)
                v_h = v_s[rs, vs]
                st = st_s[hd]
                o_h = _dot(sc, v_h) + lax.dot_general(
                    q_m, st.astype(bf16), _NT, preferred_element_type=f32)
                o_s[rs, vs] = o_h
                st_s[hd] = st * dec_p + lax.dot_general(
                    v_h, k_ep, _TN, preferred_element_type=f32)
        return carry

    lax.fori_loop(0, rows // CHUNK, chunk_body, 0)

    pg = _dot(hb, w_ref[:, c_g:c_cb])
    gn = gn_ref[...]
    for hd in range(GLA_HEADS):
        vs = slice(hd * GLA_DV, (hd + 1) * GLA_DV)
        g_h = pg[:, vs]
        mix_s[:, vs] = (_rms(o_s[:, vs], gn) * (g_h * _sigmoid(g_h))).astype(bf16)

    cb = _dot(hb, w_ref[:, c_cb:c_cc])
    u = _dot(hb, w_ref[:, c_cc:c_ch]) * _dot(hb, w_ref[:, c_ch:c_a])
    u_s[SUBLANES:SUBLANES + rows, :] = u
    cw = cw_ref[...]
    y = cw[0:1, :] * u_s[SUBLANES - 2:SUBLANES - 2 + rows, :]
    y = y + cw[1:2, :] * u_s[SUBLANES - 1:SUBLANES - 1 + rows, :]
    y = y + cw[2:3, :] * u
    mix_s[:, V_W:] = (cb * y).astype(bf16)
    u_s[0:SUBLANES, :] = u_s[rows:rows + SUBLANES, :]

    out_ref[0] = x + _dot(mix_s[...], wout_ref[...])


def _ffn_kernel(x_ref, g2_ref, wg_ref, wu_ref, wd_ref, gf_ref, out_ref, *, final_norm):
    x = x_ref[...]
    h = _rms(x, g2_ref[...]).astype(jnp.bfloat16)
    g = _dot(h, wg_ref[...])
    u = _dot(h, wu_ref[...])
    a = ((g * _sigmoid(g)) * u).astype(jnp.bfloat16)
    y = x + _dot(a, wd_ref[...])
    if final_norm:
        y = _rms(y, gf_ref[...])
    out_ref[...] = y


def _resident(shape):
    return pl.BlockSpec(shape, lambda *_: (0,) * len(shape), pipeline_mode=pl.Buffered(1))


def _mixer(x, g1, w_all, w_up, b_gate, gn, cw, w_out):
    bsz, seq, d = x.shape
    rows = MIXER_ROWS
    conv_w = cw.shape[1]
    assert seq % rows == 0 and rows % CHUNK == 0
    assert V_W + conv_w == w_out.shape[0]
    f32, bf16 = jnp.float32, jnp.bfloat16
    kern = functools.partial(_mixer_kernel, rows=rows, conv_w=conv_w)
    return pl.pallas_call(
        kern,
        grid=(bsz, seq // rows),
        in_specs=[
            pl.BlockSpec((1, rows, d), lambda b, s: (b, s, 0)),
            _resident(g1.shape), _resident(w_all.shape), _resident(w_up.shape),
            _resident(b_gate.shape), _resident(gn.shape), _resident(cw.shape),
            _resident(w_out.shape),
        ],
        out_specs=pl.BlockSpec((1, rows, d), lambda b, s: (b, s, 0)),
        out_shape=jax.ShapeDtypeStruct(x.shape, x.dtype),
        scratch_shapes=[
            pltpu.VMEM((rows, d), bf16),
            pltpu.VMEM((rows, QK_W), f32),
            pltpu.VMEM((rows, QK_W), f32),
            pltpu.VMEM((rows, QK_W), f32),
            pltpu.VMEM((rows, V_W), bf16),
            pltpu.VMEM((rows, V_W), f32),
            pltpu.VMEM((rows + SUBLANES, conv_w), f32),
            pltpu.VMEM((GLA_HEADS, GLA_DV, LANES), f32),
            pltpu.VMEM((rows, V_W + conv_w), bf16),
        ],
        compiler_params=pltpu.CompilerParams(
            dimension_semantics=("arbitrary", "arbitrary"),
            vmem_limit_bytes=VMEM_LIMIT_BYTES),
        name="mixer",
    )(x, g1, w_all, w_up, b_gate, gn, cw, w_out)


def _ffn(x2d, g2, wg, wu, wd, gf, final_norm):
    n, d = x2d.shape
    rows = FFN_ROWS
    assert n % rows == 0
    kern = functools.partial(_ffn_kernel, final_norm=final_norm)
    return pl.pallas_call(
        kern,
        grid=(n // rows,),
        in_specs=[
            pl.BlockSpec((rows, d), lambda i: (i, 0)),
            _resident(g2.shape), _resident(wg.shape), _resident(wu.shape),
            _resident(wd.shape), _resident(gf.shape),
        ],
        out_specs=pl.BlockSpec((rows, d), lambda i: (i, 0)),
        out_shape=jax.ShapeDtypeStruct(x2d.shape, x2d.dtype),
        compiler_params=pltpu.CompilerParams(
            dimension_semantics=("arbitrary",),
            vmem_limit_bytes=VMEM_LIMIT_BYTES),
        name="ffn",
    )(x2d, g2, wg, wu, wd, gf)


def kernel(x, norm1_g, w_in, w_gate_up, b_gate, gla_norm_g, conv_w, w_out, norm2_g,
           w_ffn_gate, w_ffn_up, w_ffn_down, norm_f_g):
    bsz, seq, d = x.shape
    depth = w_in.shape[0]
    bf16 = jnp.bfloat16
    a0 = 2 * QK_W + 2 * V_W
    a1 = a0 + GATE_RANK
    assert GATE_RANK <= LANES
    for l in range(depth):
        w = w_in[l]
        w_all = jnp.concatenate(
            [w[:, :a0], w[:, a1:], jnp.pad(w[:, a0:a1], ((0, 0), (0, LANES - GATE_RANK)))],
            axis=1).astype(bf16)
        w_up = jnp.pad(w_gate_up[l], ((0, LANES - GATE_RANK), (0, 0))).astype(bf16)
        x = _mixer(x, norm1_g[l][None, :], w_all, w_up, b_gate[l][None, :],
                   gla_norm_g[l][None, :], conv_w[l], w_out[l].astype(bf16))
        x = _ffn(x.reshape(bsz * seq, d), norm2_g[l][None, :],
                 w_ffn_gate[l].astype(bf16), w_ffn_up[l].astype(bf16),
                 w_ffn_down[l].astype(bf16), norm_f_g[None, :],
                 final_norm=(l == depth - 1)).reshape(bsz, seq, d)
    return x
```

```python
import functools

import jax
import jax.numpy as jnp
from jax import lax
from jax.experimental import pallas as pl
from jax.experimental.pallas import tpu as pltpu

GLA_HEADS = 4
GLA_DK = 64
GLA_DV = 128
GATE_RANK = 16
GATE_NORMALIZER = 16.0
CHUNK = 64
CONV_K = 3
NORM_EPS = 1e-6

LANES = 128
SUBLANES = 8
QK_W = GLA_HEADS * GLA_DK
V_W = GLA_HEADS * GLA_DV

MIXER_ROWS = 512
FFN_ROWS = 512
VMEM_LIMIT_BYTES = 56 * 1024 * 1024

_NT = (((1,), (1,)), ((), ()))
_TN = (((0,), (0,)), ((), ()))


def _rms(x, g):
    y = x * lax.rsqrt(jnp.mean(x * x, axis=-1, keepdims=True) + NORM_EPS)
    return y * g


def _dot(a, b):
    return jnp.dot(a, b, preferred_element_type=jnp.float32)


def _sigmoid(x):
    return 1.0 / (1.0 + jnp.exp(-x))


def _mixer_kernel(x_ref, g1_ref, w_ref, wup_ref, bg_ref, gn_ref, cw_ref, wout_ref,
                  out_ref, u_s, st_s, mix_s, *, rows, conv_w):
    f32, bf16 = jnp.float32, jnp.bfloat16
    c_q, c_v, c_g = 0, 2 * QK_W, 2 * QK_W + V_W
    c_cb = c_g + V_W
    c_cc, c_ch = c_cb + conv_w, c_cb + 2 * conv_w
    c_a = c_cb + 3 * conv_w
    pair_w = 2 * GLA_DV

    @pl.when(pl.program_id(1) == 0)
    def _():
        st_s[...] = jnp.zeros_like(st_s)
        u_s[0:SUBLANES, :] = jnp.zeros((SUBLANES, conv_w), f32)

    x = x_ref[0]
    hb = _rms(x, g1_ref[...]).astype(bf16)

    qk = _dot(hb, w_ref[:, c_q:c_v])
    v = _dot(hb, w_ref[:, c_v:c_g]).astype(bf16)
    pg = _dot(hb, w_ref[:, c_g:c_cb])
    pa = _dot(hb, w_ref[:, c_a:c_a + LANES])
    gl = _dot(pa.astype(bf16), wup_ref[...]) + bg_ref[...]
    la = (jnp.minimum(gl, 0.0) - jnp.log1p(jnp.exp(-jnp.abs(gl)))) * (1.0 / GATE_NORMALIZER)
    la_hi = la.astype(bf16)
    la_lo = (la - la_hi.astype(f32)).astype(bf16)

    row = lax.broadcasted_iota(jnp.int32, (CHUNK, LANES), 0)
    lane = lax.broadcasted_iota(jnp.int32, (CHUNK, LANES), 1)
    causal2 = (lane & (CHUNK - 1)) <= row
    tri2 = causal2.astype(bf16)
    head0_lanes = lane < GLA_DK
    lane_v = lax.broadcasted_iota(jnp.int32, (CHUNK, pair_w), 1)
    head0_vlanes = lane_v < GLA_DV
    gn = gn_ref[...]

    n_chunks = rows // CHUNK
    n_pairs = GLA_HEADS // 2
    chunk_rows = [slice(c * CHUNK, (c + 1) * CHUNK) for c in range(n_chunks)]
    pair_lanes = [slice(p * LANES, (p + 1) * LANES) for p in range(n_pairs)]
    pair_vlanes = [slice(p * pair_w, (p + 1) * pair_w) for p in range(n_pairs)]
    zk = jnp.zeros((CHUNK, LANES), bf16)
    zv = jnp.zeros((CHUNK, pair_w), bf16)

    half_w = conv_w // 2
    conv_dots = {}

    def conv_piece(name, half):
        base = {"cb": c_cb, "cc": c_cc, "ch": c_ch}[name] + half * half_w
        conv_dots[name, half] = _dot(hb, w_ref[:, base:base + half_w])

    def conv_finish(half):
        cs = slice(half * half_w, (half + 1) * half_w)
        u = conv_dots["cc", half] * conv_dots["ch", half]
        u_s[SUBLANES:SUBLANES + rows, cs] = u
        cw = cw_ref[:, cs]
        y = cw[0:1, :] * u_s[SUBLANES - 2:SUBLANES - 2 + rows, cs]
        y = y + cw[1:2, :] * u_s[SUBLANES - 1:SUBLANES - 1 + rows, cs]
        y = y + cw[2:3, :] * u
        mix_s[:, V_W + half * half_w:V_W + (half + 1) * half_w] = (
            conv_dots["cb", half] * y).astype(bf16)
        u_s[0:SUBLANES, cs] = u_s[rows:rows + SUBLANES, cs]

    b_all = [_dot(tri2, jnp.concatenate([la_hi[rs], la_lo[rs]], axis=0)) for rs in chunk_rows]
    conv_piece("cc", 0)

    q_i, k_bd, v_bd, k_eh, dec = {}, {}, {}, {}, {}
    for c, rs in enumerate(chunk_rows):
        b = b_all[c]
        b_last = b[CHUNK - 1:CHUNK, :]
        q_c = (qk[rs, :QK_W] * (GLA_DK ** -0.5) * jnp.exp(b)).astype(bf16)
        k = qk[rs, QK_W:]
        k_i = (k * jnp.exp(-b)).astype(bf16)
        k_e = (k * jnp.exp(b_last - b)).astype(bf16)
        decay = jnp.exp(b_last)
        for p in range(n_pairs):
            ls = pair_lanes[p]
            q_i[c, p] = q_c[:, ls]
            dec[c, p] = decay[:, ls]
            k_bd[c, p] = jnp.concatenate([jnp.where(head0_lanes, k_i[:, ls], zk),
                                          jnp.where(head0_lanes, zk, k_i[:, ls])], axis=0)
            v_p = v[rs, pair_vlanes[p]]
            v_bd[c, p] = jnp.concatenate([jnp.where(head0_vlanes, v_p, zv),
                                          jnp.where(head0_vlanes, zv, v_p)], axis=0)
            k_eh[c, 2 * p] = jnp.where(head0_lanes, k_e[:, ls], zk)
            k_eh[c, 2 * p + 1] = jnp.where(head0_lanes, zk, k_e[:, ls])
    conv_piece("ch", 0)

    sc, kv = {}, {}
    for c, rs in enumerate(chunk_rows):
        for p in range(n_pairs):
            sc[c, p] = lax.dot_general(q_i[c, p], k_bd[c, p], _NT, preferred_element_type=f32)
        for hd in range(GLA_HEADS):
            v_h = v[rs, hd * GLA_DV:(hd + 1) * GLA_DV]
            kv[c, hd] = lax.dot_general(v_h, k_eh[c, hd], _TN, preferred_element_type=f32)
    conv_piece("cb", 0)
    conv_finish(0)

    st = [st_s[hd] for hd in range(GLA_HEADS)]
    st_prev = {}
    for c in range(n_chunks):
        for p in range(n_pairs):
            sc[c, p] = jnp.where(causal2, sc[c, p], 0.0).astype(bf16)
            st_prev[c, p] = jnp.concatenate([st[2 * p], st[2 * p + 1]], axis=0).astype(bf16)
        for hd in range(GLA_HEADS):
            st[hd] = st[hd] * dec[c, hd // 2] + kv[c, hd]
    for hd in range(GLA_HEADS):
        st_s[hd] = st[hd]
    conv_piece("cc", 1)

    o = {}
    for c in range(n_chunks):
        for p in range(n_pairs):
            o[c, p] = _dot(sc[c, p], v_bd[c, p]) + lax.dot_general(
                q_i[c, p], st_prev[c, p], _NT, preferred_element_type=f32)
    conv_piece("ch", 1)

    for c, rs in enumerate(chunk_rows):
        for hd in range(GLA_HEADS):
            hs = slice((hd % 2) * GLA_DV, (hd % 2 + 1) * GLA_DV)
            g_h = pg[rs, hd * GLA_DV:(hd + 1) * GLA_DV]
            mix_s[rs, hd * GLA_DV:(hd + 1) * GLA_DV] = (
                _rms(o[c, hd // 2][:, hs], gn) * (g_h * _sigmoid(g_h))).astype(bf16)
    conv_piece("cb", 1)
    conv_finish(1)

    out_ref[0] = x + _dot(mix_s[...], wout_ref[...])


def _ffn_kernel(x_ref, g2_ref, wg_ref, wu_ref, wd_ref, gf_ref, out_ref, *, final_norm):
    x = x_ref[...]
    h = _rms(x, g2_ref[...]).astype(jnp.bfloat16)
    g = _dot(h, wg_ref[...])
    u = _dot(h, wu_ref[...])
    a = ((g * _sigmoid(g)) * u).astype(jnp.bfloat16)
    y = x + _dot(a, wd_ref[...])
    if final_norm:
        y = _rms(y, gf_ref[...])
    out_ref[...] = y


def _resident(shape):
    return pl.BlockSpec(shape, lambda *_: (0,) * len(shape), pipeline_mode=pl.Buffered(1))


def _mixer(x, g1, w_all, w_up, b_gate, gn, cw, w_out):
    bsz, seq, d = x.shape
    rows = MIXER_ROWS
    conv_w = cw.shape[1]
    assert seq % rows == 0 and rows % CHUNK == 0
    assert V_W + conv_w == w_out.shape[0]
    f32, bf16 = jnp.float32, jnp.bfloat16
    kern = functools.partial(_mixer_kernel, rows=rows, conv_w=conv_w)
    return pl.pallas_call(
        kern,
        grid=(bsz, seq // rows),
        in_specs=[
            pl.BlockSpec((1, rows, d), lambda b, s: (b, s, 0)),
            _resident(g1.shape), _resident(w_all.shape), _resident(w_up.shape),
            _resident(b_gate.shape), _resident(gn.shape), _resident(cw.shape),
            _resident(w_out.shape),
        ],
        out_specs=pl.BlockSpec((1, rows, d), lambda b, s: (b, s, 0)),
        out_shape=jax.ShapeDtypeStruct(x.shape, x.dtype),
        scratch_shapes=[
            pltpu.VMEM((rows + SUBLANES, conv_w), f32),
            pltpu.VMEM((GLA_HEADS, GLA_DV, LANES), f32),
            pltpu.VMEM((rows, V_W + conv_w), bf16),
        ],
        compiler_params=pltpu.CompilerParams(
            dimension_semantics=("arbitrary", "arbitrary"),
            vmem_limit_bytes=VMEM_LIMIT_BYTES),
        name="mixer",
    )(x, g1, w_all, w_up, b_gate, gn, cw, w_out)


def _ffn(x2d, g2, wg, wu, wd, gf, final_norm):
    n, d = x2d.shape
    rows = FFN_ROWS
    assert n % rows == 0
    kern = functools.partial(_ffn_kernel, final_norm=final_norm)
    return pl.pallas_call(
        kern,
        grid=(n // rows,),
        in_specs=[
            pl.BlockSpec((rows, d), lambda i: (i, 0)),
            _resident(g2.shape), _resident(wg.shape), _resident(wu.shape),
            _resident(wd.shape), _resident(gf.shape),
        ],
        out_specs=pl.BlockSpec((rows, d), lambda i: (i, 0)),
        out_shape=jax.ShapeDtypeStruct(x2d.shape, x2d.dtype),
        compiler_params=pltpu.CompilerParams(
            dimension_semantics=("arbitrary",),
            vmem_limit_bytes=VMEM_LIMIT_BYTES),
        name="ffn",
    )(x2d, g2, wg, wu, wd, gf)


def kernel(x, norm1_g, w_in, w_gate_up, b_gate, gla_norm_g, conv_w, w_out, norm2_g,
           w_ffn_gate, w_ffn_up, w_ffn_down, norm_f_g):
    bsz, seq, d = x.shape
    depth = w_in.shape[0]
    bf16 = jnp.bfloat16
    a0 = 2 * QK_W + 2 * V_W
    a1 = a0 + GATE_RANK
    assert GATE_RANK <= LANES and depth >= 1
    for l in range(depth):
        w = w_in[l]
        w_all = jnp.concatenate(
            [w[:, :a0], w[:, a1:], jnp.pad(w[:, a0:a1], ((0, 0), (0, LANES - GATE_RANK)))],
            axis=1).astype(bf16)
        w_up = jnp.pad(w_gate_up[l], ((0, LANES - GATE_RANK), (0, 0))).astype(bf16)
        x = _mixer(x, norm1_g[l][None, :], w_all, w_up, b_gate[l][None, :],
                   gla_norm_g[l][None, :], conv_w[l], w_out[l].astype(bf16))
        x = _ffn(x.reshape(bsz * seq, d), norm2_g[l][None, :],
                 w_ffn_gate[l].astype(bf16), w_ffn_up[l].astype(bf16),
                 w_ffn_down[l].astype(bf16), norm_f_g[None, :],
                 final_norm=(l == depth - 1)).reshape(bsz, seq, d)
    return x
```

```python
import functools

import jax
import jax.numpy as jnp
from jax import lax
from jax.experimental import pallas as pl
from jax.experimental.pallas import tpu as pltpu

GLA_HEADS = 4
GLA_DK = 64
GLA_DV = 128
GATE_RANK = 16
GATE_NORMALIZER = 16.0
CHUNK = 64
NORM_EPS = 1e-6

LANES = 128
SUBLANES = 8
MXU_N = 256
QK_W = GLA_HEADS * GLA_DK
V_W = GLA_HEADS * GLA_DV

TILE_ROWS = 512
VMEM_LIMIT_BYTES = 60000 * 1024

_NT = (((1,), (1,)), ((), ()))
_TN = (((0,), (0,)), ((), ()))


def _dot(a, b):
    return jnp.dot(a, b, preferred_element_type=jnp.float32)


def _sigmoid(x):
    return 1.0 / (1.0 + jnp.exp(-x))


def _rms_scale(sum_sq, width):
    return lax.rsqrt(sum_sq * (1.0 / width) + NORM_EPS)


def _rms(x, g):
    y = x * _rms_scale(jnp.sum(x * x, axis=-1, keepdims=True), x.shape[-1])
    return y * g


def _mixer_items(x_ref, g1_ref, w_ref, wup_ref, bg_ref, gn_ref, cw_ref, wout_ref, g2_ref,
                 u_s, st_s, mix_s, x1_s, h2_s, *, rows, conv_w, d_model):
    f32, bf16 = jnp.float32, jnp.bfloat16
    c_q, c_v, c_g = 0, 2 * QK_W, 2 * QK_W + V_W
    c_cb = c_g + V_W
    c_cc, c_ch = c_cb + conv_w, c_cb + 2 * conv_w
    c_a = c_cb + 3 * conv_w
    pair_w = 2 * GLA_DV
    n_chunks = rows // CHUNK
    n_pairs = GLA_HEADS // 2
    chunk_rows = [slice(c * CHUNK, (c + 1) * CHUNK) for c in range(n_chunks)]
    pair_lanes = [slice(p * LANES, (p + 1) * LANES) for p in range(n_pairs)]
    pair_vlanes = [slice(p * pair_w, (p + 1) * pair_w) for p in range(n_pairs)]

    hb = _rms(x_ref[...], g1_ref[...]).astype(bf16)
    yield "norm1"

    pa = _dot(hb, w_ref[:, c_a:c_a + LANES])
    yield "gate_a"
    qk = _dot(hb, w_ref[:, c_q:c_v])
    yield "qk"
    gl = _dot(pa.astype(bf16), wup_ref[...]) + bg_ref[...]
    la = (jnp.minimum(gl, 0.0) - jnp.log1p(jnp.exp(-jnp.abs(gl)))) * (1.0 / GATE_NORMALIZER)
    la_hi = la.astype(bf16)
    la_lo = (la - la_hi.astype(f32)).astype(bf16)
    yield "gate"
    v = _dot(hb, w_ref[:, c_v:c_g]).astype(bf16)
    yield "v"
    pg = _dot(hb, w_ref[:, c_g:c_cb])
    yield "g"

    row = lax.broadcasted_iota(jnp.int32, (CHUNK, LANES), 0)
    lane = lax.broadcasted_iota(jnp.int32, (CHUNK, LANES), 1)
    causal2 = (lane & (CHUNK - 1)) <= row
    tri2 = causal2.astype(bf16)
    head0_lanes = lane < GLA_DK
    lane_v = lax.broadcasted_iota(jnp.int32, (CHUNK, pair_w), 1)
    head0_vlanes = lane_v < GLA_DV
    zk = jnp.zeros((CHUNK, LANES), bf16)
    zv = jnp.zeros((CHUNK, pair_w), bf16)
    gn = gn_ref[...]

    half_w = conv_w // 2
    conv_dots = {}

    def conv_piece(name, half):
        base = {"cb": c_cb, "cc": c_cc, "ch": c_ch}[name] + half * half_w
        conv_dots[name, half] = _dot(hb, w_ref[:, base:base + half_w])

    def conv_finish(half):
        cs = slice(half * half_w, (half + 1) * half_w)
        u = conv_dots["cc", half] * conv_dots["ch", half]
        u_s[SUBLANES:SUBLANES + rows, cs] = u
        cw = cw_ref[:, cs]
        y = cw[0:1, :] * u_s[SUBLANES - 2:SUBLANES - 2 + rows, cs]
        y = y + cw[1:2, :] * u_s[SUBLANES - 1:SUBLANES - 1 + rows, cs]
        y = y + cw[2:3, :] * u
        mix_s[:, V_W + half * half_w:V_W + (half + 1) * half_w] = (
            conv_dots["cb", half] * y).astype(bf16)
        u_s[0:SUBLANES, cs] = u_s[rows:rows + SUBLANES, cs]

    b_all = [_dot(tri2, jnp.concatenate([la_hi[rs], la_lo[rs]], axis=0)) for rs in chunk_rows]
    yield "gla_cumsum"
    conv_piece("cc", 0)
    yield "conv"

    q_i, k_bd, v_bd, k_eh, dec = {}, {}, {}, {}, {}
    for c, rs in enumerate(chunk_rows):
        b = b_all[c]
        b_last = b[CHUNK - 1:CHUNK, :]
        q_c = (qk[rs, :QK_W] * (GLA_DK ** -0.5) * jnp.exp(b)).astype(bf16)
        k = qk[rs, QK_W:]
        k_i = (k * jnp.exp(-b)).astype(bf16)
        k_e = (k * jnp.exp(b_last - b)).astype(bf16)
        decay = jnp.exp(b_last)
        for p in range(n_pairs):
            ls = pair_lanes[p]
            q_i[c, p] = q_c[:, ls]
            dec[c, p] = decay[:, ls]
            k_bd[c, p] = jnp.concatenate([jnp.where(head0_lanes, k_i[:, ls], zk),
                                          jnp.where(head0_lanes, zk, k_i[:, ls])], axis=0)
            v_p = v[rs, pair_vlanes[p]]
            v_bd[c, p] = jnp.concatenate([jnp.where(head0_vlanes, v_p, zv),
                                          jnp.where(head0_vlanes, zv, v_p)], axis=0)
            k_eh[c, 2 * p] = jnp.where(head0_lanes, k_e[:, ls], zk)
            k_eh[c, 2 * p + 1] = jnp.where(head0_lanes, zk, k_e[:, ls])
    yield "gla_decay"
    conv_piece("ch", 0)
    yield "conv"

    sc, kv = {}, {}
    for c, rs in enumerate(chunk_rows):
        for p in range(n_pairs):
            sc[c, p] = lax.dot_general(q_i[c, p], k_bd[c, p], _NT, preferred_element_type=f32)
        for hd in range(GLA_HEADS):
            v_h = v[rs, hd * GLA_DV:(hd + 1) * GLA_DV]
            kv[c, hd] = lax.dot_general(v_h, k_eh[c, hd], _TN, preferred_element_type=f32)
    yield "gla_scores"
    conv_piece("cb", 0)
    conv_finish(0)
    yield "conv"

    st = [st_s[hd] for hd in range(GLA_HEADS)]
    st_prev = {}
    for c in range(n_chunks):
        for p in range(n_pairs):
            sc[c, p] = jnp.where(causal2, sc[c, p], 0.0).astype(bf16)
            st_prev[c, p] = jnp.concatenate([st[2 * p], st[2 * p + 1]], axis=0).astype(bf16)
        for hd in range(GLA_HEADS):
            st[hd] = st[hd] * dec[c, hd // 2] + kv[c, hd]
    for hd in range(GLA_HEADS):
        st_s[hd] = st[hd]
    yield "gla_state"
    conv_piece("cc", 1)
    yield "conv"

    o = {}
    for c in range(n_chunks):
        for p in range(n_pairs):
            o[c, p] = _dot(sc[c, p], v_bd[c, p]) + lax.dot_general(
                q_i[c, p], st_prev[c, p], _NT, preferred_element_type=f32)
    yield "gla_out"
    conv_piece("ch", 1)
    yield "conv"

    for c, rs in enumerate(chunk_rows):
        for hd in range(GLA_HEADS):
            hs = slice((hd % 2) * GLA_DV, (hd % 2 + 1) * GLA_DV)
            g_h = pg[rs, hd * GLA_DV:(hd + 1) * GLA_DV]
            mix_s[rs, hd * GLA_DV:(hd + 1) * GLA_DV] = (
                _rms(o[c, hd // 2][:, hs], gn) * (g_h * _sigmoid(g_h))).astype(bf16)
    yield "gla_gate"
    conv_piece("cb", 1)
    conv_finish(1)
    yield "conv"

    sum_sq = jnp.zeros((rows, 1), f32)
    for n in range(d_model // MXU_N):
        cs = slice(n * MXU_N, (n + 1) * MXU_N)
        x1 = x_ref[:, cs] + _dot(mix_s[...], wout_ref[:, cs])
        x1_s[:, cs] = x1
        sum_sq = sum_sq + jnp.sum(x1 * x1, axis=-1, keepdims=True)
        yield "out_proj"
    h2_s[...] = (x1_s[...] * _rms_scale(sum_sq, d_model) * g2_ref[...]).astype(bf16)
    yield "norm2"


def _ffn_items(wg_ref, wu_ref, wd_ref, gf_ref, out_ref, x1_s, xf_s, h2_s, a_s,
               *, rows, d_model, d_ff, final_norm):
    f32, bf16 = jnp.float32, jnp.bfloat16
    xf_s[...] = x1_s[...]
    yield "copy"
    for j in range(d_ff // MXU_N):
        cs = slice(j * MXU_N, (j + 1) * MXU_N)
        h2 = h2_s[...]
        g = _dot(h2, wg_ref[:, cs])
        u = _dot(h2, wu_ref[:, cs])
        a_s[:, cs] = ((g * _sigmoid(g)) * u).astype(bf16)
        yield "gate_up"
    sum_sq = jnp.zeros((rows, 1), f32)
    for n in range(d_model // MXU_N):
        cs = slice(n * MXU_N, (n + 1) * MXU_N)
        y = xf_s[:, cs] + _dot(a_s[...], wd_ref[:, cs])
        out_ref[:, cs] = y
        sum_sq = sum_sq + jnp.sum(y * y, axis=-1, keepdims=True)
        yield "down"
    if final_norm:
        out_ref[...] = out_ref[...] * _rms_scale(sum_sq, d_model) * gf_ref[...]
    yield "final_norm"


_ISSUE_ORDER = (
    "f f m "
    "f m m "
    "f m m "
    "f m m "
    "f m m "
    "f m m "
    "f m m "
    "f m m "
    "f m m "
    "f m f "
    "m f m f m f m f "
    "m f"
).split()


def _block_kernel(x_ref, g1_ref, w_ref, wup_ref, bg_ref, gn_ref, cw_ref, wout_ref,
                  g2_ref, wg_ref, wu_ref, wd_ref, gf_ref, out_ref,
                  u_s, st_s, mix_s, x1_s, xf_s, h2_s, a_s,
                  *, rows, conv_w, d_model, d_ff, tiles_per_seq, final_norm):
    t = pl.program_id(0)

    @pl.when(t == 0)
    def _():
        x1_s[...] = jnp.zeros_like(x1_s)
        h2_s[...] = jnp.zeros_like(h2_s)

    @pl.when(t % tiles_per_seq == 0)
    def _():
        st_s[...] = jnp.zeros_like(st_s)
        u_s[0:SUBLANES, :] = jnp.zeros((SUBLANES, conv_w), jnp.float32)

    streams = {
        "m": _mixer_items(x_ref, g1_ref, w_ref, wup_ref, bg_ref, gn_ref, cw_ref, wout_ref,
                          g2_ref, u_s, st_s, mix_s, x1_s, h2_s,
                          rows=rows, conv_w=conv_w, d_model=d_model),
        "f": _ffn_items(wg_ref, wu_ref, wd_ref, gf_ref, out_ref, x1_s, xf_s, h2_s, a_s,
                        rows=rows, d_model=d_model, d_ff=d_ff, final_norm=final_norm),
    }
    for which in _ISSUE_ORDER:
        next(streams[which])
    for stream in streams.values():
        assert next(stream, None) is None


def _resident(shape):
    return pl.BlockSpec(shape, lambda *_: (0,) * len(shape), pipeline_mode=pl.Buffered(1))


def _block(x2d, seq, g1, w_all, w_up, b_gate, gn, cw, w_out, g2, wg, wu, wd, gf, final_norm):
    n_rows, d = x2d.shape
    rows = TILE_ROWS
    conv_w = cw.shape[1]
    d_ff = wg.shape[1]
    assert seq % rows == 0 and rows % CHUNK == 0 and n_rows % seq == 0
    assert V_W + conv_w == w_out.shape[0] and conv_w % (2 * MXU_N) == 0
    assert d % MXU_N == 0 and d_ff % MXU_N == 0
    n_tiles = n_rows // rows
    f32, bf16 = jnp.float32, jnp.bfloat16
    kern = functools.partial(_block_kernel, rows=rows, conv_w=conv_w, d_model=d, d_ff=d_ff,
                             tiles_per_seq=seq // rows, final_norm=final_norm)
    weights = (g1, w_all, w_up, b_gate, gn, cw, w_out, g2, wg, wu, wd, gf)
    return pl.pallas_call(
        kern,
        grid=(n_tiles + 1,),
        in_specs=[pl.BlockSpec((rows, d), lambda t: (jnp.minimum(t, n_tiles - 1), 0))]
        + [_resident(w.shape) for w in weights],
        out_specs=pl.BlockSpec((rows, d), lambda t: (jnp.maximum(t - 1, 0), 0)),
        out_shape=jax.ShapeDtypeStruct(x2d.shape, x2d.dtype),
        scratch_shapes=[
            pltpu.VMEM((rows + SUBLANES, conv_w), f32),
            pltpu.VMEM((GLA_HEADS, GLA_DV, LANES), f32),
            pltpu.VMEM((rows, V_W + conv_w), bf16),
            pltpu.VMEM((rows, d), f32),
            pltpu.VMEM((rows, d), f32),
            pltpu.VMEM((rows, d), bf16),
            pltpu.VMEM((rows, d_ff), bf16),
        ],
        compiler_params=pltpu.CompilerParams(
            dimension_semantics=("arbitrary",),
            vmem_limit_bytes=VMEM_LIMIT_BYTES),
        name="block",
    )(x2d, *weights)


def kernel(x, norm1_g, w_in, w_gate_up, b_gate, gla_norm_g, conv_w, w_out, norm2_g,
           w_ffn_gate, w_ffn_up, w_ffn_down, norm_f_g):
    bsz, seq, d = x.shape
    depth = w_in.shape[0]
    bf16 = jnp.bfloat16
    a0 = 2 * QK_W + 2 * V_W
    a1 = a0 + GATE_RANK
    assert GATE_RANK <= LANES and depth >= 1
    x = x.reshape(bsz * seq, d)
    for l in range(depth):
        w = w_in[l]
        w_all = jnp.concatenate(
            [w[:, :a0], w[:, a1:], jnp.pad(w[:, a0:a1], ((0, 0), (0, LANES - GATE_RANK)))],
            axis=1).astype(bf16)
        w_up = jnp.pad(w_gate_up[l], ((0, LANES - GATE_RANK), (0, 0))).astype(bf16)
        x = _block(x, seq, norm1_g[l][None, :], w_all, w_up, b_gate[l][None, :],
                   gla_norm_g[l][None, :], conv_w[l], w_out[l].astype(bf16),
                   norm2_g[l][None, :], w_ffn_gate[l].astype(bf16), w_ffn_up[l].astype(bf16),
                   w_ffn_down[l].astype(bf16), norm_f_g[None, :],
                   final_norm=(l == depth - 1))
    return x.reshape(bsz, seq, d)
```

```python
import functools

import jax
import jax.numpy as jnp
from jax import lax
from jax.experimental import pallas as pl
from jax.experimental.pallas import tpu as pltpu

GLA_HEADS = 4
GLA_DK = 64
GLA_DV = 128
GATE_RANK = 16
GATE_NORMALIZER = 16.0
CHUNK = 64
NORM_EPS = 1e-6

LANES = 128
SUBLANES = 8
MXU_N = 256
QK_W = GLA_HEADS * GLA_DK
V_W = GLA_HEADS * GLA_DV

TILE_ROWS = 512
VMEM_LIMIT_BYTES = 60000 * 1024

_NT = (((1,), (1,)), ((), ()))
_TN = (((0,), (0,)), ((), ()))


def _dot(a, b):
    return jnp.dot(a, b, preferred_element_type=jnp.float32)


def _sigmoid(x):
    return 1.0 / (1.0 + jnp.exp(-x))


def _rms_scale(sum_sq, width):
    return lax.rsqrt(sum_sq * (1.0 / width) + NORM_EPS)


def _rms(x, g):
    y = x * _rms_scale(jnp.sum(x * x, axis=-1, keepdims=True), x.shape[-1])
    return y * g


def _mixer_items(x_ref, g1_ref, w_ref, wup_ref, bg_ref, gn_ref, cw_ref, wout_ref, g2_ref,
                 u_s, st_s, mix_s, x1_s, h2_s, *, rows, conv_w, d_model):
    f32, bf16 = jnp.float32, jnp.bfloat16
    c_q, c_v, c_g = 0, 2 * QK_W, 2 * QK_W + V_W
    c_cb = c_g + V_W
    c_cc, c_ch = c_cb + conv_w, c_cb + 2 * conv_w
    c_a = c_cb + 3 * conv_w
    pair_w = 2 * GLA_DV
    n_chunks = rows // CHUNK
    n_pairs = GLA_HEADS // 2
    chunk_rows = [slice(c * CHUNK, (c + 1) * CHUNK) for c in range(n_chunks)]
    pair_lanes = [slice(p * LANES, (p + 1) * LANES) for p in range(n_pairs)]
    pair_vlanes = [slice(p * pair_w, (p + 1) * pair_w) for p in range(n_pairs)]

    hb = _rms(x_ref[...], g1_ref[...]).astype(bf16)
    yield "norm1"

    pa = _dot(hb, w_ref[:, c_a:c_a + LANES])
    yield "gate_a"
    qk = _dot(hb, w_ref[:, c_q:c_v])
    yield "qk"
    gl = _dot(pa.astype(bf16), wup_ref[...]) + bg_ref[...]
    la = (jnp.minimum(gl, 0.0) - jnp.log1p(jnp.exp(-jnp.abs(gl)))) * (1.0 / GATE_NORMALIZER)
    la_hi = la.astype(bf16)
    la_lo = (la - la_hi.astype(f32)).astype(bf16)
    yield "gate"
    v = _dot(hb, w_ref[:, c_v:c_g]).astype(bf16)
    yield "v"
    pg = _dot(hb, w_ref[:, c_g:c_cb])
    yield "g"

    row = lax.broadcasted_iota(jnp.int32, (CHUNK, LANES), 0)
    lane = lax.broadcasted_iota(jnp.int32, (CHUNK, LANES), 1)
    tri2 = ((lane & (CHUNK - 1)) <= row).astype(bf16)
    head0_lanes = lane < GLA_DK
    row_q = lax.broadcasted_iota(jnp.int32, (CHUNK, QK_W), 0)
    lane_q = lax.broadcasted_iota(jnp.int32, (CHUNK, QK_W), 1)
    causal4 = (lane_q & (CHUNK - 1)) <= row_q
    head_lanes = [(lane_q >= hd * GLA_DK) & (lane_q < (hd + 1) * GLA_DK) for hd in range(GLA_HEADS)]
    zq = jnp.zeros((CHUNK, QK_W), bf16)
    lane_v = lax.broadcasted_iota(jnp.int32, (CHUNK, pair_w), 1)
    head0_vlanes = lane_v < GLA_DV
    zk = jnp.zeros((CHUNK, LANES), bf16)
    zv = jnp.zeros((CHUNK, pair_w), bf16)
    gn = gn_ref[...]

    half_w = conv_w // 2
    conv_dots = {}

    def conv_piece(name, half):
        base = {"cb": c_cb, "cc": c_cc, "ch": c_ch}[name] + half * half_w
        conv_dots[name, half] = _dot(hb, w_ref[:, base:base + half_w])

    def conv_finish(half):
        cs = slice(half * half_w, (half + 1) * half_w)
        u = conv_dots["cc", half] * conv_dots["ch", half]
        u_s[SUBLANES:SUBLANES + rows, cs] = u
        cw = cw_ref[:, cs]
        y = cw[0:1, :] * u_s[SUBLANES - 2:SUBLANES - 2 + rows, cs]
        y = y + cw[1:2, :] * u_s[SUBLANES - 1:SUBLANES - 1 + rows, cs]
        y = y + cw[2:3, :] * u
        mix_s[:, V_W + half * half_w:V_W + (half + 1) * half_w] = (
            conv_dots["cb", half] * y).astype(bf16)
        u_s[0:SUBLANES, cs] = u_s[rows:rows + SUBLANES, cs]

    b_all = [_dot(tri2, jnp.concatenate([la_hi[rs], la_lo[rs]], axis=0)) for rs in chunk_rows]
    yield "gla_cumsum"
    conv_piece("cc", 0)
    yield "conv"

    q_i, k_bd, k_e_bd, v_bd, dec_col = {}, {}, {}, {}, {}
    for c, rs in enumerate(chunk_rows):
        b = b_all[c]
        b_last = b[CHUNK - 1:CHUNK, :]
        q_i[c] = (qk[rs, :QK_W] * (GLA_DK ** -0.5) * jnp.exp(b)).astype(bf16)
        k = qk[rs, QK_W:]
        k_i = (k * jnp.exp(-b)).astype(bf16)
        k_e = (k * jnp.exp(b_last - b)).astype(bf16)
        k_bd[c] = jnp.concatenate(
            [jnp.where(head_lanes[hd], k_i, zq) for hd in range(GLA_HEADS)], axis=0)
        b_tail = b[CHUNK - SUBLANES:CHUNK, :]
        for p in range(n_pairs):
            ls = pair_lanes[p]
            dec_col[c, p] = jnp.exp(b_tail[:, ls].T[:, SUBLANES - 1:SUBLANES])
            k_e_bd[c, p] = jnp.concatenate([jnp.where(head0_lanes, k_e[:, ls], zk),
                                            jnp.where(head0_lanes, zk, k_e[:, ls])], axis=0)
            v_p = v[rs, pair_vlanes[p]]
            v_bd[c, p] = jnp.concatenate([jnp.where(head0_vlanes, v_p, zv),
                                          jnp.where(head0_vlanes, zv, v_p)], axis=0)
    yield "gla_decay"
    conv_piece("ch", 0)
    yield "conv"

    sc, kv = {}, {}
    for c in range(n_chunks):
        sc[c] = lax.dot_general(q_i[c], k_bd[c], _NT, preferred_element_type=f32)
        for p in range(n_pairs):
            kv[c, p] = lax.dot_general(k_e_bd[c, p], v_bd[c, p], _TN, preferred_element_type=f32)
    yield "gla_scores"
    conv_piece("cb", 0)
    conv_finish(0)
    yield "conv"

    st = [st_s[p] for p in range(n_pairs)]
    st_prev = {}
    for c in range(n_chunks):
        sc[c] = jnp.where(causal4, sc[c], 0.0).astype(bf16)
        for p in range(n_pairs):
            st_prev[c, p] = st[p].astype(bf16)
            st[p] = st[p] * dec_col[c, p] + kv[c, p]
    for p in range(n_pairs):
        st_s[p] = st[p]
    yield "gla_state"
    conv_piece("cc", 1)
    yield "conv"

    o = {}
    for c in range(n_chunks):
        for p in range(n_pairs):
            ls = pair_lanes[p]
            lhs = jnp.concatenate([sc[c][:, ls], q_i[c][:, ls]], axis=1)
            rhs = jnp.concatenate([v_bd[c, p], st_prev[c, p]], axis=0)
            o[c, p] = _dot(lhs, rhs)
    yield "gla_out"
    conv_piece("ch", 1)
    yield "conv"

    for c, rs in enumerate(chunk_rows):
        for hd in range(GLA_HEADS):
            hs = slice((hd % 2) * GLA_DV, (hd % 2 + 1) * GLA_DV)
            g_h = pg[rs, hd * GLA_DV:(hd + 1) * GLA_DV]
            mix_s[rs, hd * GLA_DV:(hd + 1) * GLA_DV] = (
                _rms(o[c, hd // 2][:, hs], gn) * (g_h * _sigmoid(g_h))).astype(bf16)
    yield "gla_gate"
    conv_piece("cb", 1)
    conv_finish(1)
    yield "conv"

    sum_sq = jnp.zeros((rows, 1), f32)
    for n in range(d_model // MXU_N):
        cs = slice(n * MXU_N, (n + 1) * MXU_N)
        x1 = x_ref[:, cs] + _dot(mix_s[...], wout_ref[:, cs])
        x1_s[:, cs] = x1
        sum_sq = sum_sq + jnp.sum(x1 * x1, axis=-1, keepdims=True)
        yield "out_proj"
    h2_s[...] = (x1_s[...] * _rms_scale(sum_sq, d_model) * g2_ref[...]).astype(bf16)
    yield "norm2"


def _ffn_items(wg_ref, wu_ref, wd_ref, gf_ref, out_ref, x1_s, xf_s, h2_s, a_s,
               *, rows, d_model, d_ff, final_norm):
    f32, bf16 = jnp.float32, jnp.bfloat16
    for j in range(d_ff // MXU_N):
        if j == 2:
            xf_s[...] = x1_s[...]
            yield "copy"
        cs = slice(j * MXU_N, (j + 1) * MXU_N)
        h2 = h2_s[...]
        g = _dot(h2, wg_ref[:, cs])
        u = _dot(h2, wu_ref[:, cs])
        a_s[:, cs] = ((g * _sigmoid(g)) * u).astype(bf16)
        yield "gate_up"
    sum_sq = jnp.zeros((rows, 1), f32)
    for n in range(d_model // MXU_N):
        cs = slice(n * MXU_N, (n + 1) * MXU_N)
        y = xf_s[:, cs] + _dot(a_s[...], wd_ref[:, cs])
        out_ref[:, cs] = y
        sum_sq = sum_sq + jnp.sum(y * y, axis=-1, keepdims=True)
        yield "down"
    if final_norm:
        out_ref[...] = out_ref[...] * _rms_scale(sum_sq, d_model) * gf_ref[...]
    yield "final_norm"


_ISSUE_ORDER = (
    "f m "
    "f m m "
    "f f m m "
    "f m m "
    "f m m "
    "f m m "
    "f m m "
    "f m m "
    "f m m "
    "f m "
    "m f m "
    "f m f m "
    "m f f f"
).split()


def _block_kernel(x_ref, g1_ref, w_ref, wup_ref, bg_ref, gn_ref, cw_ref, wout_ref,
                  g2_ref, wg_ref, wu_ref, wd_ref, gf_ref, out_ref,
                  u_s, st_s, mix_s, x1_s, xf_s, h2_s, a_s,
                  *, rows, conv_w, d_model, d_ff, tiles_per_seq, final_norm):
    t = pl.program_id(0)

    @pl.when(t == 0)
    def _():
        x1_s[...] = jnp.zeros_like(x1_s)
        h2_s[...] = jnp.zeros_like(h2_s)

    @pl.when(t % tiles_per_seq == 0)
    def _():
        st_s[...] = jnp.zeros_like(st_s)
        u_s[0:SUBLANES, :] = jnp.zeros((SUBLANES, conv_w), jnp.float32)

    streams = {
        "m": _mixer_items(x_ref, g1_ref, w_ref, wup_ref, bg_ref, gn_ref, cw_ref, wout_ref,
                          g2_ref, u_s, st_s, mix_s, x1_s, h2_s,
                          rows=rows, conv_w=conv_w, d_model=d_model),
        "f": _ffn_items(wg_ref, wu_ref, wd_ref, gf_ref, out_ref, x1_s, xf_s, h2_s, a_s,
                        rows=rows, d_model=d_model, d_ff=d_ff, final_norm=final_norm),
    }
    for which in _ISSUE_ORDER:
        next(streams[which])
    for stream in streams.values():
        assert next(stream, None) is None


def _resident(shape):
    return pl.BlockSpec(shape, lambda *_: (0,) * len(shape), pipeline_mode=pl.Buffered(1))


def _block(x2d, seq, g1, w_all, w_up, b_gate, gn, cw, w_out, g2, wg, wu, wd, gf, final_norm):
    n_rows, d = x2d.shape
    rows = TILE_ROWS
    conv_w = cw.shape[1]
    d_ff = wg.shape[1]
    assert seq % rows == 0 and rows % CHUNK == 0 and n_rows % seq == 0
    assert V_W + conv_w == w_out.shape[0] and conv_w % (2 * MXU_N) == 0
    assert d % MXU_N == 0 and d_ff % MXU_N == 0
    n_tiles = n_rows // rows
    f32, bf16 = jnp.float32, jnp.bfloat16
    kern = functools.partial(_block_kernel, rows=rows, conv_w=conv_w, d_model=d, d_ff=d_ff,
                             tiles_per_seq=seq // rows, final_norm=final_norm)
    weights = (g1, w_all, w_up, b_gate, gn, cw, w_out, g2, wg, wu, wd, gf)
    return pl.pallas_call(
        kern,
        grid=(n_tiles + 1,),
        in_specs=[pl.BlockSpec((rows, d), lambda t: (jnp.minimum(t, n_tiles - 1), 0))]
        + [_resident(w.shape) for w in weights],
        out_specs=pl.BlockSpec((rows, d), lambda t: (jnp.maximum(t - 1, 0), 0)),
        out_shape=jax.ShapeDtypeStruct(x2d.shape, x2d.dtype),
        scratch_shapes=[
            pltpu.VMEM((rows + SUBLANES, conv_w), f32),
            pltpu.VMEM((GLA_HEADS // 2, LANES, 2 * GLA_DV), f32),
            pltpu.VMEM((rows, V_W + conv_w), bf16),
            pltpu.VMEM((rows, d), f32),
            pltpu.VMEM((rows, d), f32),
            pltpu.VMEM((rows, d), bf16),
            pltpu.VMEM((rows, d_ff), bf16),
        ],
        compiler_params=pltpu.CompilerParams(
            dimension_semantics=("arbitrary",),
            vmem_limit_bytes=VMEM_LIMIT_BYTES),
        name="block",
    )(x2d, *weights)


def kernel(x, norm1_g, w_in, w_gate_up, b_gate, gla_norm_g, conv_w, w_out, norm2_g,
           w_ffn_gate, w_ffn_up, w_ffn_down, norm_f_g):
    bsz, seq, d = x.shape
    depth = w_in.shape[0]
    bf16 = jnp.bfloat16
    a0 = 2 * QK_W + 2 * V_W
    a1 = a0 + GATE_RANK
    assert GATE_RANK <= LANES and depth >= 1
    x = x.reshape(bsz * seq, d)
    for l in range(depth):
        w = w_in[l]
        w_all = jnp.concatenate(
            [w[:, :a0], w[:, a1:], jnp.pad(w[:, a0:a1], ((0, 0), (0, LANES - GATE_RANK)))],
            axis=1).astype(bf16)
        w_up = jnp.pad(w_gate_up[l], ((0, LANES - GATE_RANK), (0, 0))).astype(bf16)
        x = _block(x, seq, norm1_g[l][None, :], w_all, w_up, b_gate[l][None, :],
                   gla_norm_g[l][None, :], conv_w[l], w_out[l].astype(bf16),
                   norm2_g[l][None, :], w_ffn_gate[l].astype(bf16), w_ffn_up[l].astype(bf16),
                   w_ffn_down[l].astype(bf16), norm_f_g[None, :],
                   final_norm=(l == depth - 1))
    return x.reshape(bsz, seq, d)
```

```python
import functools

import jax
import jax.numpy as jnp
from jax import lax
from jax.experimental import pallas as pl
from jax.experimental.pallas import tpu as pltpu

GLA_HEADS = 4
GLA_DK = 64
GLA_DV = 128
GATE_RANK = 16
GATE_NORMALIZER = 16.0
CHUNK = 64
NORM_EPS = 1e-6

LANES = 128
SUBLANES = 8
MXU_N = 256
QK_W = GLA_HEADS * GLA_DK
V_W = GLA_HEADS * GLA_DV

TILE_ROWS = 512
PREP_ROWS = 128
PREP_SLOTS = 4
VMEM_LIMIT_BYTES = 60000 * 1024

_NT = (((1,), (1,)), ((), ()))
_TN = (((0,), (0,)), ((), ()))


def _dot(a, b):
    return jnp.dot(a, b, preferred_element_type=jnp.float32)


def _sigmoid(x):
    return 1.0 / (1.0 + jnp.exp(-x))


def _rms_scale(sum_sq, width):
    return lax.rsqrt(sum_sq * (1.0 / width) + NORM_EPS)


def _rms(x, g):
    y = x * _rms_scale(jnp.sum(x * x, axis=-1, keepdims=True), x.shape[-1])
    return y * g


def _mixer_items(x_ref, g1_ref, w_ref, wup_ref, bg_ref, gn_ref, cw_ref, wout_ref, g2_ref,
                 u_s, st_s, mix_s, x1_s, h2_s, *, rows, conv_w, d_model):
    f32, bf16 = jnp.float32, jnp.bfloat16
    c_q, c_v, c_g = 0, 2 * QK_W, 2 * QK_W + V_W
    c_cb = c_g + V_W
    c_cc, c_ch = c_cb + conv_w, c_cb + 2 * conv_w
    c_a = c_cb + 3 * conv_w
    pair_w = 2 * GLA_DV
    n_chunks = rows // CHUNK
    n_pairs = GLA_HEADS // 2
    chunk_rows = [slice(c * CHUNK, (c + 1) * CHUNK) for c in range(n_chunks)]
    pair_lanes = [slice(p * LANES, (p + 1) * LANES) for p in range(n_pairs)]
    pair_vlanes = [slice(p * pair_w, (p + 1) * pair_w) for p in range(n_pairs)]

    hb = _rms(x_ref[...], g1_ref[...]).astype(bf16)
    yield "norm1"

    pa = _dot(hb, w_ref[:, c_a:c_a + LANES])
    yield "gate_a"
    qk = _dot(hb, w_ref[:, c_q:c_v])
    yield "qk"
    gl = _dot(pa.astype(bf16), wup_ref[...]) + bg_ref[...]
    la = (jnp.minimum(gl, 0.0) - jnp.log1p(jnp.exp(-jnp.abs(gl)))) * (1.0 / GATE_NORMALIZER)
    la_hi = la.astype(bf16)
    la_lo = (la - la_hi.astype(f32)).astype(bf16)
    yield "gate"
    v = _dot(hb, w_ref[:, c_v:c_g]).astype(bf16)
    yield "v"
    pg = _dot(hb, w_ref[:, c_g:c_cb])
    yield "g"

    row = lax.broadcasted_iota(jnp.int32, (CHUNK, LANES), 0)
    lane = lax.broadcasted_iota(jnp.int32, (CHUNK, LANES), 1)
    tri2 = ((lane & (CHUNK - 1)) <= row).astype(bf16)
    head0_lanes = lane < GLA_DK
    row_q = lax.broadcasted_iota(jnp.int32, (CHUNK, QK_W), 0)
    lane_q = lax.broadcasted_iota(jnp.int32, (CHUNK, QK_W), 1)
    causal4 = (lane_q & (CHUNK - 1)) <= row_q
    head_lanes = [(lane_q >= hd * GLA_DK) & (lane_q < (hd + 1) * GLA_DK) for hd in range(GLA_HEADS)]
    zq = jnp.zeros((CHUNK, QK_W), bf16)
    lane_v = lax.broadcasted_iota(jnp.int32, (CHUNK, pair_w), 1)
    head0_vlanes = lane_v < GLA_DV
    zk = jnp.zeros((CHUNK, LANES), bf16)
    zv = jnp.zeros((CHUNK, pair_w), bf16)
    gn = gn_ref[...]

    half_w = conv_w // 2
    conv_dots = {}

    def conv_piece(name, half):
        base = {"cb": c_cb, "cc": c_cc, "ch": c_ch}[name] + half * half_w
        conv_dots[name, half] = _dot(hb, w_ref[:, base:base + half_w])

    def conv_finish(half):
        cs = slice(half * half_w, (half + 1) * half_w)
        u = conv_dots["cc", half] * conv_dots["ch", half]
        u_s[SUBLANES:SUBLANES + rows, cs] = u
        cw = cw_ref[:, cs]
        y = cw[0:1, :] * u_s[SUBLANES - 2:SUBLANES - 2 + rows, cs]
        y = y + cw[1:2, :] * u_s[SUBLANES - 1:SUBLANES - 1 + rows, cs]
        y = y + cw[2:3, :] * u
        mix_s[:, V_W + half * half_w:V_W + (half + 1) * half_w] = (
            conv_dots["cb", half] * y).astype(bf16)
        u_s[0:SUBLANES, cs] = u_s[rows:rows + SUBLANES, cs]

    b_all = [_dot(tri2, jnp.concatenate([la_hi[rs], la_lo[rs]], axis=0)) for rs in chunk_rows]
    yield "gla_cumsum"
    conv_piece("cc", 0)
    yield "conv"

    q_i, k_bd, k_e_bd, v_bd, dec_col = {}, {}, {}, {}, {}
    for c, rs in enumerate(chunk_rows):
        b = b_all[c]
        b_last = b[CHUNK - 1:CHUNK, :]
        q_i[c] = (qk[rs, :QK_W] * (GLA_DK ** -0.5) * jnp.exp(b)).astype(bf16)
        k = qk[rs, QK_W:]
        k_i = (k * jnp.exp(-b)).astype(bf16)
        k_e = (k * jnp.exp(b_last - b)).astype(bf16)
        k_bd[c] = jnp.concatenate(
            [jnp.where(head_lanes[hd], k_i, zq) for hd in range(GLA_HEADS)], axis=0)
        b_tail = b[CHUNK - SUBLANES:CHUNK, :]
        for p in range(n_pairs):
            ls = pair_lanes[p]
            dec_col[c, p] = jnp.exp(b_tail[:, ls].T[:, SUBLANES - 1:SUBLANES])
            k_e_bd[c, p] = jnp.concatenate([jnp.where(head0_lanes, k_e[:, ls], zk),
                                            jnp.where(head0_lanes, zk, k_e[:, ls])], axis=0)
            v_p = v[rs, pair_vlanes[p]]
            v_bd[c, p] = jnp.concatenate([jnp.where(head0_vlanes, v_p, zv),
                                          jnp.where(head0_vlanes, zv, v_p)], axis=0)
    yield "gla_decay"
    conv_piece("ch", 0)
    yield "conv"

    sc, kv = {}, {}
    for c in range(n_chunks):
        sc[c] = lax.dot_general(q_i[c], k_bd[c], _NT, preferred_element_type=f32)
        for p in range(n_pairs):
            kv[c, p] = lax.dot_general(k_e_bd[c, p], v_bd[c, p], _TN, preferred_element_type=f32)
    yield "gla_scores"
    conv_piece("cb", 0)
    conv_finish(0)
    yield "conv"

    st = [st_s[p] for p in range(n_pairs)]
    st_prev = {}
    for c in range(n_chunks):
        sc[c] = jnp.where(causal4, sc[c], 0.0).astype(bf16)
        for p in range(n_pairs):
            st_prev[c, p] = st[p].astype(bf16)
            st[p] = st[p] * dec_col[c, p] + kv[c, p]
    for p in range(n_pairs):
        st_s[p] = st[p]
    yield "gla_state"
    conv_piece("cc", 1)
    yield "conv"

    o = {}
    for c in range(n_chunks):
        for p in range(n_pairs):
            ls = pair_lanes[p]
            lhs = jnp.concatenate([sc[c][:, ls], q_i[c][:, ls]], axis=1)
            rhs = jnp.concatenate([v_bd[c, p], st_prev[c, p]], axis=0)
            o[c, p] = _dot(lhs, rhs)
    yield "gla_out"
    conv_piece("ch", 1)
    yield "conv"

    for c, rs in enumerate(chunk_rows):
        for hd in range(GLA_HEADS):
            hs = slice((hd % 2) * GLA_DV, (hd % 2 + 1) * GLA_DV)
            g_h = pg[rs, hd * GLA_DV:(hd + 1) * GLA_DV]
            mix_s[rs, hd * GLA_DV:(hd + 1) * GLA_DV] = (
                _rms(o[c, hd // 2][:, hs], gn) * (g_h * _sigmoid(g_h))).astype(bf16)
    yield "gla_gate"
    conv_piece("cb", 1)
    conv_finish(1)
    yield "conv"

    sum_sq = jnp.zeros((rows, 1), f32)
    for n in range(d_model // MXU_N):
        cs = slice(n * MXU_N, (n + 1) * MXU_N)
        x1 = x_ref[:, cs] + _dot(mix_s[...], wout_ref[:, cs])
        x1_s[:, cs] = x1
        sum_sq = sum_sq + jnp.sum(x1 * x1, axis=-1, keepdims=True)
        yield "out_proj"
    h2_s[...] = (x1_s[...] * _rms_scale(sum_sq, d_model) * g2_ref[...]).astype(bf16)
    yield "norm2"


def _ffn_items(wg_ref, wu_ref, wd_ref, gf_ref, out_ref, x1_s, xf_s, h2_s, a_s,
               *, rows, d_model, d_ff, final_norm):
    f32, bf16 = jnp.float32, jnp.bfloat16
    for j in range(d_ff // MXU_N):
        if j == 2:
            xf_s[...] = x1_s[...]
            yield "copy"
        cs = slice(j * MXU_N, (j + 1) * MXU_N)
        h2 = h2_s[...]
        g = _dot(h2, wg_ref[:, cs])
        u = _dot(h2, wu_ref[:, cs])
        a_s[:, cs] = ((g * _sigmoid(g)) * u).astype(bf16)
        yield "gate_up"
    sum_sq = jnp.zeros((rows, 1), f32)
    for n in range(d_model // MXU_N):
        cs = slice(n * MXU_N, (n + 1) * MXU_N)
        y = xf_s[:, cs] + _dot(a_s[...], wd_ref[:, cs])
        out_ref[:, cs] = y
        sum_sq = sum_sq + jnp.sum(y * y, axis=-1, keepdims=True)
        yield "down"
    if final_norm:
        out_ref[...] = out_ref[...] * _rms_scale(sum_sq, d_model) * gf_ref[...]
    yield "final_norm"


_ISSUE_ORDER = (
    "f m "
    "f m m "
    "f f m m "
    "f m m "
    "f m m "
    "f m m "
    "f m m "
    "f m m "
    "f m m "
    "f m "
    "m f m "
    "f m f m "
    "m f f f"
).split()


def _stream_rows(src_hbm, consume):
    n_rows, width = src_hbm.shape
    assert n_rows % PREP_ROWS == 0
    n = n_rows // PREP_ROWS
    ahead = PREP_SLOTS - 1

    def body(stage, sem):
        def copy(i, slot):
            return pltpu.make_async_copy(
                src_hbm.at[pl.ds(i * PREP_ROWS, PREP_ROWS), :], stage.at[slot], sem.at[slot])

        for i in range(min(ahead, n)):
            copy(i, i).start()

        def step(i, carry):
            slot = i % PREP_SLOTS
            copy(i, slot).wait()

            @pl.when(i + ahead < n)
            def _():
                copy(i + ahead, (i + ahead) % PREP_SLOTS).start()

            consume(pl.ds(pl.multiple_of(i * PREP_ROWS, PREP_ROWS), PREP_ROWS), stage[slot])
            return carry

        lax.fori_loop(0, n, step, 0)

    pl.run_scoped(body, pltpu.VMEM((PREP_SLOTS, PREP_ROWS, width), jnp.float32),
                  pltpu.SemaphoreType.DMA((PREP_SLOTS,)))


def _prepare_weights(w_in_hbm, wup_ref, wout_hbm, wg_hbm, wu_hbm, wd_hbm,
                     w_s, wup_s, wout_s, wg_s, wu_s, wd_s):
    bf16 = jnp.bfloat16
    a0 = 2 * QK_W + 2 * V_W
    a1 = a0 + GATE_RANK
    n_in = w_in_hbm.shape[1]

    def put_w_in(rs, blk):
        w_s[rs, 0:a0] = blk[:, 0:a0].astype(bf16)
        w_s[rs, a0:a0 + n_in - a1] = blk[:, a1:n_in].astype(bf16)
        pad = jnp.zeros((PREP_ROWS, LANES - GATE_RANK), jnp.float32)
        w_s[rs, a0 + n_in - a1:] = jnp.concatenate([blk[:, a0:a1], pad], axis=1).astype(bf16)

    def put(dst):
        def consume(rs, blk):
            dst[rs, :] = blk.astype(bf16)
        return consume

    _stream_rows(w_in_hbm, put_w_in)
    _stream_rows(wout_hbm, put(wout_s))
    _stream_rows(wg_hbm, put(wg_s))
    _stream_rows(wu_hbm, put(wu_s))
    _stream_rows(wd_hbm, put(wd_s))
    wup_s[...] = jnp.zeros_like(wup_s)
    wup_s[0:GATE_RANK, :] = wup_ref[...].astype(bf16)


def _block_kernel(x_ref, g1_ref, w_in_hbm, wup_ref, bg_ref, gn_ref, cw_ref, wout_hbm,
                  g2_ref, wg_hbm, wu_hbm, wd_hbm, gf_ref, out_ref,
                  u_s, st_s, mix_s, x1_s, xf_s, h2_s, a_s,
                  w_s, wup_s, wout_s, wg_s, wu_s, wd_s,
                  *, rows, conv_w, d_model, d_ff, tiles_per_seq, final_norm):
    t = pl.program_id(0)

    @pl.when(t == 0)
    def _():
        _prepare_weights(w_in_hbm, wup_ref, wout_hbm, wg_hbm, wu_hbm, wd_hbm,
                         w_s, wup_s, wout_s, wg_s, wu_s, wd_s)
        x1_s[...] = jnp.zeros_like(x1_s)
        h2_s[...] = jnp.zeros_like(h2_s)

    @pl.when(t % tiles_per_seq == 0)
    def _():
        st_s[...] = jnp.zeros_like(st_s)
        u_s[0:SUBLANES, :] = jnp.zeros((SUBLANES, conv_w), jnp.float32)

    streams = {
        "m": _mixer_items(x_ref, g1_ref, w_s, wup_s, bg_ref, gn_ref, cw_ref, wout_s,
                          g2_ref, u_s, st_s, mix_s, x1_s, h2_s,
                          rows=rows, conv_w=conv_w, d_model=d_model),
        "f": _ffn_items(wg_s, wu_s, wd_s, gf_ref, out_ref, x1_s, xf_s, h2_s, a_s,
                        rows=rows, d_model=d_model, d_ff=d_ff, final_norm=final_norm),
    }
    for which in _ISSUE_ORDER:
        next(streams[which])
    for stream in streams.values():
        assert next(stream, None) is None


def _resident(shape):
    return pl.BlockSpec(shape, lambda *_: (0,) * len(shape), pipeline_mode=pl.Buffered(1))


def _block(x2d, seq, g1, w_in, w_up, b_gate, gn, cw, w_out, g2, wg, wu, wd, gf, final_norm):
    n_rows, d = x2d.shape
    rows = TILE_ROWS
    conv_w = cw.shape[1]
    d_ff = wg.shape[1]
    n_in = w_in.shape[1] - GATE_RANK + LANES
    assert seq % rows == 0 and rows % CHUNK == 0 and n_rows % seq == 0
    assert V_W + conv_w == w_out.shape[0] and conv_w % (2 * MXU_N) == 0
    assert w_in.shape[1] == 2 * QK_W + 2 * V_W + GATE_RANK + 3 * conv_w
    assert d % MXU_N == 0 and d_ff % MXU_N == 0 and GATE_RANK <= LANES
    n_tiles = n_rows // rows
    f32, bf16 = jnp.float32, jnp.bfloat16
    kern = functools.partial(_block_kernel, rows=rows, conv_w=conv_w, d_model=d, d_ff=d_ff,
                             tiles_per_seq=seq // rows, final_norm=final_norm)
    operands = (g1, w_in, w_up, b_gate, gn, cw, w_out, g2, wg, wu, wd, gf)
    in_hbm = {id(w) for w in (w_in, w_out, wg, wu, wd)}
    specs = [pl.BlockSpec(memory_space=pl.ANY) if id(w) in in_hbm else _resident(w.shape)
             for w in operands]
    return pl.pallas_call(
        kern,
        grid=(n_tiles + 1,),
        in_specs=[pl.BlockSpec((rows, d), lambda t: (jnp.minimum(t, n_tiles - 1), 0))] + specs,
        out_specs=pl.BlockSpec((rows, d), lambda t: (jnp.maximum(t - 1, 0), 0)),
        out_shape=jax.ShapeDtypeStruct(x2d.shape, x2d.dtype),
        scratch_shapes=[
            pltpu.VMEM((rows + SUBLANES, conv_w), f32),
            pltpu.VMEM((GLA_HEADS // 2, LANES, 2 * GLA_DV), f32),
            pltpu.VMEM((rows, V_W + conv_w), bf16),
            pltpu.VMEM((rows, d), f32),
            pltpu.VMEM((rows, d), f32),
            pltpu.VMEM((rows, d), bf16),
            pltpu.VMEM((rows, d_ff), bf16),
            pltpu.VMEM((d, n_in), bf16),
            pltpu.VMEM((LANES, QK_W), bf16),
            pltpu.VMEM(w_out.shape, bf16),
            pltpu.VMEM(wg.shape, bf16),
            pltpu.VMEM(wu.shape, bf16),
            pltpu.VMEM(wd.shape, bf16),
        ],
        compiler_params=pltpu.CompilerParams(
            dimension_semantics=("arbitrary",),
            vmem_limit_bytes=VMEM_LIMIT_BYTES),
        name="block",
    )(x2d, *operands)


def kernel(x, norm1_g, w_in, w_gate_up, b_gate, gla_norm_g, conv_w, w_out, norm2_g,
           w_ffn_gate, w_ffn_up, w_ffn_down, norm_f_g):
    bsz, seq, d = x.shape
    depth = w_in.shape[0]
    assert depth >= 1
    x = x.reshape(bsz * seq, d)
    for l in range(depth):
        x = _block(x, seq, norm1_g[l][None, :], w_in[l], w_gate_up[l], b_gate[l][None, :],
                   gla_norm_g[l][None, :], conv_w[l], w_out[l], norm2_g[l][None, :],
                   w_ffn_gate[l], w_ffn_up[l], w_ffn_down[l], norm_f_g[None, :],
                   final_norm=(l == depth - 1))
    return x.reshape(bsz, seq, d)
```

```python
import functools

import jax
import jax.numpy as jnp
from jax import lax
from jax.experimental import pallas as pl
from jax.experimental.pallas import tpu as pltpu

GLA_HEADS = 4
GLA_DK = 64
GLA_DV = 128
GATE_RANK = 16
GATE_NORMALIZER = 16.0
CHUNK = 64
NORM_EPS = 1e-6

LANES = 128
SUBLANES = 8
MXU_N = 256
QK_W = GLA_HEADS * GLA_DK
V_W = GLA_HEADS * GLA_DV

TILE_ROWS = 512
PREP_ROWS = 128
PREP_SLOTS = 4
VMEM_LIMIT_BYTES = 60000 * 1024

_NT = (((1,), (1,)), ((), ()))
_TN = (((0,), (0,)), ((), ()))


def _dot(a, b):
    return jnp.dot(a, b, preferred_element_type=jnp.float32)


def _sigmoid(x):
    return 1.0 / (1.0 + jnp.exp(-x))


def _rms_scale(sum_sq, width):
    return lax.rsqrt(sum_sq * (1.0 / width) + NORM_EPS)


def _rms(x, g):
    y = x * _rms_scale(jnp.sum(x * x, axis=-1, keepdims=True), x.shape[-1])
    return y * g


def _mixer_items(x_ref, g1_ref, w_ref, wup_ref, bg_ref, gn_ref, cw_ref, wout_ref, g2_ref,
                 u_s, st_s, mix_s, x1_s, h2_s, *, rows, conv_w, d_model):
    f32, bf16 = jnp.float32, jnp.bfloat16
    c_q, c_v, c_g = 0, 2 * QK_W, 2 * QK_W + V_W
    c_cb = c_g + V_W
    c_cc, c_ch = c_cb + conv_w, c_cb + 2 * conv_w
    c_a = c_cb + 3 * conv_w
    pair_w = 2 * GLA_DV
    n_chunks = rows // CHUNK
    n_pairs = GLA_HEADS // 2
    chunk_rows = [slice(c * CHUNK, (c + 1) * CHUNK) for c in range(n_chunks)]
    pair_lanes = [slice(p * LANES, (p + 1) * LANES) for p in range(n_pairs)]
    pair_vlanes = [slice(p * pair_w, (p + 1) * pair_w) for p in range(n_pairs)]

    hb = _rms(x_ref[...], g1_ref[...]).astype(bf16)
    yield "norm1"

    pa = _dot(hb, w_ref[:, c_a:c_a + LANES])
    yield "gate_a"
    qk = _dot(hb, w_ref[:, c_q:c_v])
    yield "qk"
    gl = _dot(pa.astype(bf16), wup_ref[...]) + bg_ref[...]
    la = (jnp.minimum(gl, 0.0) - jnp.log1p(jnp.exp(-jnp.abs(gl)))) * (1.0 / GATE_NORMALIZER)
    la_hi = la.astype(bf16)
    la_lo = (la - la_hi.astype(f32)).astype(bf16)
    yield "gate"
    v = _dot(hb, w_ref[:, c_v:c_g]).astype(bf16)
    yield "v"
    pg = _dot(hb, w_ref[:, c_g:c_cb])
    yield "g"

    row = lax.broadcasted_iota(jnp.int32, (CHUNK, LANES), 0)
    lane = lax.broadcasted_iota(jnp.int32, (CHUNK, LANES), 1)
    tri2 = ((lane & (CHUNK - 1)) <= row).astype(bf16)
    head0_lanes = lane < GLA_DK
    row_q = lax.broadcasted_iota(jnp.int32, (CHUNK, QK_W), 0)
    lane_q = lax.broadcasted_iota(jnp.int32, (CHUNK, QK_W), 1)
    causal4 = (lane_q & (CHUNK - 1)) <= row_q
    head_lanes = [(lane_q >= hd * GLA_DK) & (lane_q < (hd + 1) * GLA_DK) for hd in range(GLA_HEADS)]
    zq = jnp.zeros((CHUNK, QK_W), bf16)
    lane_v = lax.broadcasted_iota(jnp.int32, (CHUNK, pair_w), 1)
    head0_vlanes = lane_v < GLA_DV
    zk = jnp.zeros((CHUNK, LANES), bf16)
    zv = jnp.zeros((CHUNK, pair_w), bf16)
    gn = gn_ref[...]

    half_w = conv_w // 2
    conv_dots = {}

    def conv_piece(name, half):
        base = {"cb": c_cb, "cc": c_cc, "ch": c_ch}[name] + half * half_w
        conv_dots[name, half] = _dot(hb, w_ref[:, base:base + half_w])

    def conv_finish(half):
        cs = slice(half * half_w, (half + 1) * half_w)
        u = conv_dots["cc", half] * conv_dots["ch", half]
        u_s[SUBLANES:SUBLANES + rows, cs] = u
        cw = cw_ref[:, cs]
        y = cw[0:1, :] * u_s[SUBLANES - 2:SUBLANES - 2 + rows, cs]
        y = y + cw[1:2, :] * u_s[SUBLANES - 1:SUBLANES - 1 + rows, cs]
        y = y + cw[2:3, :] * u
        mix_s[:, V_W + half * half_w:V_W + (half + 1) * half_w] = (
            conv_dots["cb", half] * y).astype(bf16)
        u_s[0:SUBLANES, cs] = u_s[rows:rows + SUBLANES, cs]

    b_all = [_dot(tri2, jnp.concatenate([la_hi[rs], la_lo[rs]], axis=0)) for rs in chunk_rows]
    yield "gla_cumsum"
    conv_piece("cc", 0)
    yield "conv"

    q_i, k_bd, k_e_bd, v_bd, dec_col = {}, {}, {}, {}, {}
    for c, rs in enumerate(chunk_rows):
        b = b_all[c]
        b_last = b[CHUNK - 1:CHUNK, :]
        q_i[c] = (qk[rs, :QK_W] * (GLA_DK ** -0.5) * jnp.exp(b)).astype(bf16)
        k = qk[rs, QK_W:]
        k_i = (k * jnp.exp(-b)).astype(bf16)
        k_e = (k * jnp.exp(b_last - b)).astype(bf16)
        k_bd[c] = jnp.concatenate(
            [jnp.where(head_lanes[hd], k_i, zq) for hd in range(GLA_HEADS)], axis=0)
        b_tail = b[CHUNK - SUBLANES:CHUNK, :]
        for p in range(n_pairs):
            ls = pair_lanes[p]
            dec_col[c, p] = jnp.exp(b_tail[:, ls].T[:, SUBLANES - 1:SUBLANES])
            k_e_bd[c, p] = jnp.concatenate([jnp.where(head0_lanes, k_e[:, ls], zk),
                                            jnp.where(head0_lanes, zk, k_e[:, ls])], axis=0)
            v_p = v[rs, pair_vlanes[p]]
            v_bd[c, p] = jnp.concatenate([jnp.where(head0_vlanes, v_p, zv),
                                          jnp.where(head0_vlanes, zv, v_p)], axis=0)
    yield "gla_decay"
    conv_piece("ch", 0)
    yield "conv"

    sc, kv = {}, {}
    for c in range(n_chunks):
        sc[c] = lax.dot_general(q_i[c], k_bd[c], _NT, preferred_element_type=f32)
        for p in range(n_pairs):
            kv[c, p] = lax.dot_general(k_e_bd[c, p], v_bd[c, p], _TN, preferred_element_type=f32)
    yield "gla_scores"
    conv_piece("cb", 0)
    conv_finish(0)
    yield "conv"

    st = [st_s[p] for p in range(n_pairs)]
    st_prev = {}
    for c in range(n_chunks):
        sc[c] = jnp.where(causal4, sc[c], 0.0).astype(bf16)
        for p in range(n_pairs):
            st_prev[c, p] = st[p].astype(bf16)
            st[p] = st[p] * dec_col[c, p] + kv[c, p]
    for p in range(n_pairs):
        st_s[p] = st[p]
    yield "gla_state"
    conv_piece("cc", 1)
    yield "conv"

    o = {}
    for c in range(n_chunks):
        for p in range(n_pairs):
            ls = pair_lanes[p]
            lhs = jnp.concatenate([sc[c][:, ls], q_i[c][:, ls]], axis=1)
            rhs = jnp.concatenate([v_bd[c, p], st_prev[c, p]], axis=0)
            o[c, p] = _dot(lhs, rhs)
    yield "gla_out"
    conv_piece("ch", 1)
    yield "conv"

    for c, rs in enumerate(chunk_rows):
        for hd in range(GLA_HEADS):
            hs = slice((hd % 2) * GLA_DV, (hd % 2 + 1) * GLA_DV)
            g_h = pg[rs, hd * GLA_DV:(hd + 1) * GLA_DV]
            mix_s[rs, hd * GLA_DV:(hd + 1) * GLA_DV] = (
                _rms(o[c, hd // 2][:, hs], gn) * (g_h * _sigmoid(g_h))).astype(bf16)
    yield "gla_gate"
    conv_piece("cb", 1)
    conv_finish(1)
    yield "conv"

    sum_sq = jnp.zeros((rows, 1), f32)
    for n in range(d_model // MXU_N):
        cs = slice(n * MXU_N, (n + 1) * MXU_N)
        x1 = x_ref[:, cs] + _dot(mix_s[...], wout_ref[:, cs])
        x1_s[:, cs] = x1
        sum_sq = sum_sq + jnp.sum(x1 * x1, axis=-1, keepdims=True)
        yield "out_proj"
    h2_s[...] = (x1_s[...] * _rms_scale(sum_sq, d_model) * g2_ref[...]).astype(bf16)
    yield "norm2"


def _ffn_items(wg_ref, wu_ref, wd_ref, gf_ref, out_ref, x1_s, xf_s, h2_s, a_s,
               *, rows, d_model, d_ff, final_norm):
    f32, bf16 = jnp.float32, jnp.bfloat16
    for j in range(d_ff // MXU_N):
        if j == 2:
            xf_s[...] = x1_s[...]
            yield "copy"
        cs = slice(j * MXU_N, (j + 1) * MXU_N)
        h2 = h2_s[...]
        g = _dot(h2, wg_ref[:, cs])
        u = _dot(h2, wu_ref[:, cs])
        a_s[:, cs] = ((g * _sigmoid(g)) * u).astype(bf16)
        yield "gate_up"
    sum_sq = jnp.zeros((rows, 1), f32)
    for n in range(d_model // MXU_N):
        cs = slice(n * MXU_N, (n + 1) * MXU_N)
        y = xf_s[:, cs] + _dot(a_s[...], wd_ref[:, cs])
        out_ref[:, cs] = y
        sum_sq = sum_sq + jnp.sum(y * y, axis=-1, keepdims=True)
        yield "down"
    if final_norm:
        out_ref[...] = out_ref[...] * _rms_scale(sum_sq, d_model) * gf_ref[...]
    yield "final_norm"


_ISSUE_ORDER = (
    "f m "
    "f m m "
    "f f m m "
    "f m m "
    "f m m "
    "f m m "
    "f m m "
    "f m m "
    "f m m "
    "f m "
    "m f m "
    "f m f m "
    "m f f f"
).split()


def _stream_rows(src_hbm, consume):
    n_rows, width = src_hbm.shape
    assert n_rows % PREP_ROWS == 0
    n = n_rows // PREP_ROWS
    ahead = PREP_SLOTS - 1

    def body(stage, sem):
        def copy(i, slot):
            return pltpu.make_async_copy(
                src_hbm.at[pl.ds(i * PREP_ROWS, PREP_ROWS), :], stage.at[slot], sem.at[slot])

        for i in range(min(ahead, n)):
            copy(i, i).start()

        def step(i, carry):
            slot = i % PREP_SLOTS
            copy(i, slot).wait()

            @pl.when(i + ahead < n)
            def _():
                copy(i + ahead, (i + ahead) % PREP_SLOTS).start()

            consume(pl.ds(pl.multiple_of(i * PREP_ROWS, PREP_ROWS), PREP_ROWS), stage[slot])
            return carry

        lax.fori_loop(0, n, step, 0)

    pl.run_scoped(body, pltpu.VMEM((PREP_SLOTS, PREP_ROWS, width), jnp.float32),
                  pltpu.SemaphoreType.DMA((PREP_SLOTS,)))


def _prepare_w_in(w_in_t_hbm, w_s):
    n_cols, d_model = w_in_t_hbm.shape
    a0 = 2 * QK_W + 2 * V_W
    a1 = a0 + GATE_RANK
    assert a0 % LANES == 0 and (n_cols - a1) % LANES == 0 and a0 + LANES <= n_cols
    blocks = [(r, r, LANES) for r in range(0, a0, LANES)]
    blocks += [(a1 + r, a0 + r, LANES) for r in range(0, n_cols - a1, LANES)]
    blocks += [(a0, a0 + n_cols - a1, GATE_RANK)]
    ahead = PREP_SLOTS - 1
    lane = lax.broadcasted_iota(jnp.int32, (d_model, LANES), 1)

    def body(stage, sem):
        def copy(j):
            slot = j % PREP_SLOTS
            return pltpu.make_async_copy(
                w_in_t_hbm.at[pl.ds(blocks[j][0], LANES), :], stage.at[slot], sem.at[slot])

        for j in range(min(ahead, len(blocks))):
            copy(j).start()
        for j, (_, dst, keep) in enumerate(blocks):
            copy(j).wait()
            if j + ahead < len(blocks):
                copy(j + ahead).start()
            blk = stage[j % PREP_SLOTS].T
            if keep < LANES:
                blk = jnp.where(lane < keep, blk, 0.0)
            w_s[:, dst:dst + LANES] = blk.astype(jnp.bfloat16)

    pl.run_scoped(body, pltpu.VMEM((PREP_SLOTS, LANES, d_model), jnp.float32),
                  pltpu.SemaphoreType.DMA((PREP_SLOTS,)))


def _prepare_weights(w_in_t_hbm, wup_ref, wout_hbm, wg_hbm, wu_hbm, wd_hbm,
                     w_s, wup_s, wout_s, wg_s, wu_s, wd_s):
    bf16 = jnp.bfloat16

    def put(dst):
        def consume(rs, blk):
            dst[rs, :] = blk.astype(bf16)
        return consume

    _prepare_w_in(w_in_t_hbm, w_s)
    _stream_rows(wout_hbm, put(wout_s))
    _stream_rows(wg_hbm, put(wg_s))
    _stream_rows(wu_hbm, put(wu_s))
    _stream_rows(wd_hbm, put(wd_s))
    wup_s[...] = jnp.zeros_like(wup_s)
    wup_s[0:GATE_RANK, :] = wup_ref[...].astype(bf16)


def _block_kernel(x_ref, g1_ref, w_in_t_hbm, wup_ref, bg_ref, gn_ref, cw_ref, wout_hbm,
                  g2_ref, wg_hbm, wu_hbm, wd_hbm, gf_ref, out_ref,
                  u_s, st_s, mix_s, x1_s, xf_s, h2_s, a_s,
                  w_s, wup_s, wout_s, wg_s, wu_s, wd_s,
                  *, rows, conv_w, d_model, d_ff, tiles_per_seq, final_norm):
    t = pl.program_id(0)

    @pl.when(t == 0)
    def _():
        _prepare_weights(w_in_t_hbm, wup_ref, wout_hbm, wg_hbm, wu_hbm, wd_hbm,
                         w_s, wup_s, wout_s, wg_s, wu_s, wd_s)
        x1_s[...] = jnp.zeros_like(x1_s)
        h2_s[...] = jnp.zeros_like(h2_s)

    @pl.when(t % tiles_per_seq == 0)
    def _():
        st_s[...] = jnp.zeros_like(st_s)
        u_s[0:SUBLANES, :] = jnp.zeros((SUBLANES, conv_w), jnp.float32)

    streams = {
        "m": _mixer_items(x_ref, g1_ref, w_s, wup_s, bg_ref, gn_ref, cw_ref, wout_s,
                          g2_ref, u_s, st_s, mix_s, x1_s, h2_s,
                          rows=rows, conv_w=conv_w, d_model=d_model),
        "f": _ffn_items(wg_s, wu_s, wd_s, gf_ref, out_ref, x1_s, xf_s, h2_s, a_s,
                        rows=rows, d_model=d_model, d_ff=d_ff, final_norm=final_norm),
    }
    for which in _ISSUE_ORDER:
        next(streams[which])
    for stream in streams.values():
        assert next(stream, None) is None


def _resident(shape):
    return pl.BlockSpec(shape, lambda *_: (0,) * len(shape), pipeline_mode=pl.Buffered(1))


def _block(x2d, seq, g1, w_in_t, w_up, b_gate, gn, cw, w_out, g2, wg, wu, wd, gf, final_norm):
    n_rows, d = x2d.shape
    rows = TILE_ROWS
    conv_w = cw.shape[1]
    d_ff = wg.shape[1]
    n_in = w_in_t.shape[0] - GATE_RANK + LANES
    assert seq % rows == 0 and rows % CHUNK == 0 and n_rows % seq == 0
    assert V_W + conv_w == w_out.shape[0] and conv_w % (2 * MXU_N) == 0
    assert w_in_t.shape[0] == 2 * QK_W + 2 * V_W + GATE_RANK + 3 * conv_w
    assert d % MXU_N == 0 and d_ff % MXU_N == 0 and GATE_RANK <= LANES
    n_tiles = n_rows // rows
    f32, bf16 = jnp.float32, jnp.bfloat16
    kern = functools.partial(_block_kernel, rows=rows, conv_w=conv_w, d_model=d, d_ff=d_ff,
                             tiles_per_seq=seq // rows, final_norm=final_norm)
    operands = (g1, w_in_t, w_up, b_gate, gn, cw, w_out, g2, wg, wu, wd, gf)
    in_hbm = {id(w) for w in (w_in_t, w_out, wg, wu, wd)}
    specs = [pl.BlockSpec(memory_space=pl.ANY) if id(w) in in_hbm else _resident(w.shape)
             for w in operands]
    return pl.pallas_call(
        kern,
        grid=(n_tiles + 1,),
        in_specs=[pl.BlockSpec((rows, d), lambda t: (jnp.minimum(t, n_tiles - 1), 0))] + specs,
        out_specs=pl.BlockSpec((rows, d), lambda t: (jnp.maximum(t - 1, 0), 0)),
        out_shape=jax.ShapeDtypeStruct(x2d.shape, x2d.dtype),
        scratch_shapes=[
            pltpu.VMEM((rows + SUBLANES, conv_w), f32),
            pltpu.VMEM((GLA_HEADS // 2, LANES, 2 * GLA_DV), f32),
            pltpu.VMEM((rows, V_W + conv_w), bf16),
            pltpu.VMEM((rows, d), f32),
            pltpu.VMEM((rows, d), f32),
            pltpu.VMEM((rows, d), bf16),
            pltpu.VMEM((rows, d_ff), bf16),
            pltpu.VMEM((d, n_in), bf16),
            pltpu.VMEM((LANES, QK_W), bf16),
            pltpu.VMEM(w_out.shape, bf16),
            pltpu.VMEM(wg.shape, bf16),
            pltpu.VMEM(wu.shape, bf16),
            pltpu.VMEM(wd.shape, bf16),
        ],
        compiler_params=pltpu.CompilerParams(
            dimension_semantics=("arbitrary",),
            vmem_limit_bytes=VMEM_LIMIT_BYTES),
        name="block",
    )(x2d, *operands)


def kernel(x, norm1_g, w_in, w_gate_up, b_gate, gla_norm_g, conv_w, w_out, norm2_g,
           w_ffn_gate, w_ffn_up, w_ffn_down, norm_f_g):
    bsz, seq, d = x.shape
    depth = w_in.shape[0]
    assert depth >= 1
    x = x.reshape(bsz * seq, d)
    for l in range(depth):
        w_in_t = jnp.swapaxes(w_in[l], 0, 1)
        x = _block(x, seq, norm1_g[l][None, :], w_in_t, w_gate_up[l], b_gate[l][None, :],
                   gla_norm_g[l][None, :], conv_w[l], w_out[l], norm2_g[l][None, :],
                   w_ffn_gate[l], w_ffn_up[l], w_ffn_down[l], norm_f_g[None, :],
                   final_norm=(l == depth - 1))
    return x.reshape(bsz, seq, d)
```

```python
import functools

import jax
import jax.numpy as jnp
from jax import lax
from jax.experimental import pallas as pl
from jax.experimental.pallas import tpu as pltpu

GLA_HEADS = 4
GLA_DK = 64
GLA_DV = 128
GATE_RANK = 16
GATE_NORMALIZER = 16.0
CHUNK = 64
NORM_EPS = 1e-6

LANES = 128
SUBLANES = 8
MXU_N = 256
QK_W = GLA_HEADS * GLA_DK
V_W = GLA_HEADS * GLA_DV

TILE_ROWS = 512
PREP_DMA_BYTES = 3 * 512 * 1024
PREP_SLOTS = 4
PREP_SLOTS_T = 8
VMEM_LIMIT_BYTES = 60000 * 1024

_NT = (((1,), (1,)), ((), ()))
_TN = (((0,), (0,)), ((), ()))


def _dot(a, b):
    return jnp.dot(a, b, preferred_element_type=jnp.float32)


def _sigmoid(x):
    return 1.0 / (1.0 + jnp.exp(-x))


def _rms_scale(sum_sq, width):
    return lax.rsqrt(sum_sq * (1.0 / width) + NORM_EPS)


def _rms(x, g):
    y = x * _rms_scale(jnp.sum(x * x, axis=-1, keepdims=True), x.shape[-1])
    return y * g


def _mixer_items(x_ref, g1_ref, w_ref, wup_ref, bg_ref, gn_ref, cw_ref, wout_ref, g2_ref,
                 u_s, st_s, mix_s, x1_s, h2_s, *, rows, conv_w, d_model):
    f32, bf16 = jnp.float32, jnp.bfloat16
    c_q, c_v, c_g = 0, 2 * QK_W, 2 * QK_W + V_W
    c_cb = c_g + V_W
    c_cc, c_ch = c_cb + conv_w, c_cb + 2 * conv_w
    c_a = c_cb + 3 * conv_w
    pair_w = 2 * GLA_DV
    n_chunks = rows // CHUNK
    n_pairs = GLA_HEADS // 2
    chunk_rows = [slice(c * CHUNK, (c + 1) * CHUNK) for c in range(n_chunks)]
    pair_lanes = [slice(p * LANES, (p + 1) * LANES) for p in range(n_pairs)]
    pair_vlanes = [slice(p * pair_w, (p + 1) * pair_w) for p in range(n_pairs)]

    hb = _rms(x_ref[...], g1_ref[...]).astype(bf16)
    yield "norm1"

    pa = _dot(hb, w_ref[:, c_a:c_a + LANES])
    yield "gate_a"
    qk = _dot(hb, w_ref[:, c_q:c_v])
    yield "qk"
    gl = _dot(pa.astype(bf16), wup_ref[...]) + bg_ref[...]
    la = (jnp.minimum(gl, 0.0) - jnp.log1p(jnp.exp(-jnp.abs(gl)))) * (1.0 / GATE_NORMALIZER)
    la_hi = la.astype(bf16)
    la_lo = (la - la_hi.astype(f32)).astype(bf16)
    yield "gate"
    v = _dot(hb, w_ref[:, c_v:c_g]).astype(bf16)
    yield "v"
    pg = _dot(hb, w_ref[:, c_g:c_cb])
    yield "g"

    row = lax.broadcasted_iota(jnp.int32, (CHUNK, LANES), 0)
    lane = lax.broadcasted_iota(jnp.int32, (CHUNK, LANES), 1)
    tri2 = ((lane & (CHUNK - 1)) <= row).astype(bf16)
    head0_lanes = lane < GLA_DK
    row_q = lax.broadcasted_iota(jnp.int32, (CHUNK, QK_W), 0)
    lane_q = lax.broadcasted_iota(jnp.int32, (CHUNK, QK_W), 1)
    causal4 = (lane_q & (CHUNK - 1)) <= row_q
    head_lanes = [(lane_q >= hd * GLA_DK) & (lane_q < (hd + 1) * GLA_DK) for hd in range(GLA_HEADS)]
    zq = jnp.zeros((CHUNK, QK_W), bf16)
    lane_v = lax.broadcasted_iota(jnp.int32, (CHUNK, pair_w), 1)
    head0_vlanes = lane_v < GLA_DV
    zk = jnp.zeros((CHUNK, LANES), bf16)
    zv = jnp.zeros((CHUNK, pair_w), bf16)
    gn = gn_ref[...]

    half_w = conv_w // 2
    conv_dots = {}

    def conv_piece(name, half):
        base = {"cb": c_cb, "cc": c_cc, "ch": c_ch}[name] + half * half_w
        conv_dots[name, half] = _dot(hb, w_ref[:, base:base + half_w])

    def conv_finish(half):
        cs = slice(half * half_w, (half + 1) * half_w)
        u = conv_dots["cc", half] * conv_dots["ch", half]
        u_s[SUBLANES:SUBLANES + rows, cs] = u
        cw = cw_ref[:, cs]
        y = cw[0:1, :] * u_s[SUBLANES - 2:SUBLANES - 2 + rows, cs]
        y = y + cw[1:2, :] * u_s[SUBLANES - 1:SUBLANES - 1 + rows, cs]
        y = y + cw[2:3, :] * u
        mix_s[:, V_W + half * half_w:V_W + (half + 1) * half_w] = (
            conv_dots["cb", half] * y).astype(bf16)
        u_s[0:SUBLANES, cs] = u_s[rows:rows + SUBLANES, cs]

    b_all = [_dot(tri2, jnp.concatenate([la_hi[rs], la_lo[rs]], axis=0)) for rs in chunk_rows]
    yield "gla_cumsum"
    conv_piece("cc", 0)
    yield "conv"

    q_i, k_bd, k_e_bd, v_bd, dec_col = {}, {}, {}, {}, {}
    for c, rs in enumerate(chunk_rows):
        b = b_all[c]
        b_last = b[CHUNK - 1:CHUNK, :]
        q_i[c] = (qk[rs, :QK_W] * (GLA_DK ** -0.5) * jnp.exp(b)).astype(bf16)
        k = qk[rs, QK_W:]
        k_i = (k * jnp.exp(-b)).astype(bf16)
        k_e = (k * jnp.exp(b_last - b)).astype(bf16)
        k_bd[c] = jnp.concatenate(
            [jnp.where(head_lanes[hd], k_i, zq) for hd in range(GLA_HEADS)], axis=0)
        b_tail = b[CHUNK - SUBLANES:CHUNK, :]
        for p in range(n_pairs):
            ls = pair_lanes[p]
            dec_col[c, p] = jnp.exp(b_tail[:, ls].T[:, SUBLANES - 1:SUBLANES])
            k_e_bd[c, p] = jnp.concatenate([jnp.where(head0_lanes, k_e[:, ls], zk),
                                            jnp.where(head0_lanes, zk, k_e[:, ls])], axis=0)
            v_p = v[rs, pair_vlanes[p]]
            v_bd[c, p] = jnp.concatenate([jnp.where(head0_vlanes, v_p, zv),
                                          jnp.where(head0_vlanes, zv, v_p)], axis=0)
    yield "gla_decay"
    conv_piece("ch", 0)
    yield "conv"

    sc, kv = {}, {}
    for c in range(n_chunks):
        sc[c] = lax.dot_general(q_i[c], k_bd[c], _NT, preferred_element_type=f32)
        for p in range(n_pairs):
            kv[c, p] = lax.dot_general(k_e_bd[c, p], v_bd[c, p], _TN, preferred_element_type=f32)
    yield "gla_scores"
    conv_piece("cb", 0)
    conv_finish(0)
    yield "conv"

    st = [st_s[p] for p in range(n_pairs)]
    st_prev = {}
    for c in range(n_chunks):
        sc[c] = jnp.where(causal4, sc[c], 0.0).astype(bf16)
        for p in range(n_pairs):
            st_prev[c, p] = st[p].astype(bf16)
            st[p] = st[p] * dec_col[c, p] + kv[c, p]
    for p in range(n_pairs):
        st_s[p] = st[p]
    yield "gla_state"
    conv_piece("cc", 1)
    yield "conv"

    o = {}
    for c in range(n_chunks):
        for p in range(n_pairs):
            ls = pair_lanes[p]
            lhs = jnp.concatenate([sc[c][:, ls], q_i[c][:, ls]], axis=1)
            rhs = jnp.concatenate([v_bd[c, p], st_prev[c, p]], axis=0)
            o[c, p] = _dot(lhs, rhs)
    yield "gla_out"
    conv_piece("ch", 1)
    yield "conv"

    for c, rs in enumerate(chunk_rows):
        for hd in range(GLA_HEADS):
            hs = slice((hd % 2) * GLA_DV, (hd % 2 + 1) * GLA_DV)
            g_h = pg[rs, hd * GLA_DV:(hd + 1) * GLA_DV]
            mix_s[rs, hd * GLA_DV:(hd + 1) * GLA_DV] = (
                _rms(o[c, hd // 2][:, hs], gn) * (g_h * _sigmoid(g_h))).astype(bf16)
    yield "gla_gate"
    conv_piece("cb", 1)
    conv_finish(1)
    yield "conv"

    sum_sq = jnp.zeros((rows, 1), f32)
    for n in range(d_model // MXU_N):
        cs = slice(n * MXU_N, (n + 1) * MXU_N)
        x1 = x_ref[:, cs] + _dot(mix_s[...], wout_ref[:, cs])
        x1_s[:, cs] = x1
        sum_sq = sum_sq + jnp.sum(x1 * x1, axis=-1, keepdims=True)
        yield "out_proj"
    h2_s[...] = (x1_s[...] * _rms_scale(sum_sq, d_model) * g2_ref[...]).astype(bf16)
    yield "norm2"


def _ffn_items(wg_ref, wu_ref, wd_ref, gf_ref, out_ref, x1_s, xf_s, h2_s, a_s,
               *, rows, d_model, d_ff, final_norm):
    f32, bf16 = jnp.float32, jnp.bfloat16
    for j in range(d_ff // MXU_N):
        if j == 2:
            xf_s[...] = x1_s[...]
            yield "copy"
        cs = slice(j * MXU_N, (j + 1) * MXU_N)
        h2 = h2_s[...]
        g = _dot(h2, wg_ref[:, cs])
        u = _dot(h2, wu_ref[:, cs])
        a_s[:, cs] = ((g * _sigmoid(g)) * u).astype(bf16)
        yield "gate_up"
    sum_sq = jnp.zeros((rows, 1), f32)
    for n in range(d_model // MXU_N):
        cs = slice(n * MXU_N, (n + 1) * MXU_N)
        y = xf_s[:, cs] + _dot(a_s[...], wd_ref[:, cs])
        out_ref[:, cs] = y
        sum_sq = sum_sq + jnp.sum(y * y, axis=-1, keepdims=True)
        yield "down"
    if final_norm:
        out_ref[...] = out_ref[...] * _rms_scale(sum_sq, d_model) * gf_ref[...]
    yield "final_norm"


_ISSUE_ORDER = (
    "f m "
    "f m m "
    "f f m m "
    "f m m "
    "f m m "
    "f m m "
    "f m m "
    "f m m "
    "f m m "
    "f m "
    "m f m "
    "f m f m "
    "m f f f"
).split()


def _stream_rows(src_hbm, consume):
    n_rows, width = src_hbm.shape
    rows = SUBLANES
    while 2 * rows * width * 4 <= PREP_DMA_BYTES and n_rows % (2 * rows) == 0:
        rows *= 2
    assert n_rows % rows == 0
    n = n_rows // rows
    ahead = PREP_SLOTS - 1

    def body(stage, sem):
        def copy(i, slot):
            return pltpu.make_async_copy(
                src_hbm.at[pl.ds(i * rows, rows), :], stage.at[slot], sem.at[slot])

        for i in range(min(ahead, n)):
            copy(i, i).start()

        def step(i, carry):
            slot = i % PREP_SLOTS
            copy(i, slot).wait()

            @pl.when(i + ahead < n)
            def _():
                copy(i + ahead, (i + ahead) % PREP_SLOTS).start()

            consume(pl.ds(pl.multiple_of(i * rows, rows), rows), stage[slot])
            return carry

        lax.fori_loop(0, n, step, 0)

    pl.run_scoped(body, pltpu.VMEM((PREP_SLOTS, rows, width), jnp.float32),
                  pltpu.SemaphoreType.DMA((PREP_SLOTS,)))


def _prepare_w_in(w_in_t_hbm, w_s):
    n_cols, d_model = w_in_t_hbm.shape
    a0 = 2 * QK_W + 2 * V_W
    a1 = a0 + GATE_RANK
    assert a0 % LANES == 0 and (n_cols - a1) % LANES == 0 and a0 + LANES <= n_cols
    blocks = [(r, r, LANES) for r in range(0, a0, LANES)]
    blocks += [(a1 + r, a0 + r, LANES) for r in range(0, n_cols - a1, LANES)]
    blocks += [(a0, a0 + n_cols - a1, GATE_RANK)]
    ahead = PREP_SLOTS_T - 1
    lane = lax.broadcasted_iota(jnp.int32, (d_model, LANES), 1)

    def body(stage, sem):
        def copy(j):
            slot = j % PREP_SLOTS_T
            return pltpu.make_async_copy(
                w_in_t_hbm.at[pl.ds(blocks[j][0], LANES), :], stage.at[slot], sem.at[slot])

        for j in range(min(ahead, len(blocks))):
            copy(j).start()
        for j, (_, dst, keep) in enumerate(blocks):
            copy(j).wait()
            if j + ahead < len(blocks):
                copy(j + ahead).start()
            blk = stage[j % PREP_SLOTS_T].T
            if keep < LANES:
                blk = jnp.where(lane < keep, blk, 0.0)
            w_s[:, dst:dst + LANES] = blk.astype(jnp.bfloat16)

    pl.run_scoped(body, pltpu.VMEM((PREP_SLOTS_T, LANES, d_model), jnp.float32),
                  pltpu.SemaphoreType.DMA((PREP_SLOTS_T,)))


def _prepare_weights(w_in_t_hbm, wup_ref, wout_hbm, wg_hbm, wu_hbm, wd_hbm,
                     w_s, wup_s, wout_s, wg_s, wu_s, wd_s):
    bf16 = jnp.bfloat16

    def put(dst):
        def consume(rs, blk):
            dst[rs, :] = blk.astype(bf16)
        return consume

    _prepare_w_in(w_in_t_hbm, w_s)
    _stream_rows(wout_hbm, put(wout_s))
    _stream_rows(wg_hbm, put(wg_s))
    _stream_rows(wu_hbm, put(wu_s))
    _stream_rows(wd_hbm, put(wd_s))
    wup_s[...] = jnp.zeros_like(wup_s)
    wup_s[0:GATE_RANK, :] = wup_ref[...].astype(bf16)


def _block_kernel(x_ref, g1_ref, w_in_t_hbm, wup_ref, bg_ref, gn_ref, cw_ref, wout_hbm,
                  g2_ref, wg_hbm, wu_hbm, wd_hbm, gf_ref, out_ref,
                  u_s, st_s, mix_s, x1_s, xf_s, h2_s, a_s,
                  w_s, wup_s, wout_s, wg_s, wu_s, wd_s,
                  *, rows, conv_w, d_model, d_ff, tiles_per_seq, n_tiles, final_norm):
    t = pl.program_id(0)

    @pl.when(t == 0)
    def _():
        _prepare_weights(w_in_t_hbm, wup_ref, wout_hbm, wg_hbm, wu_hbm, wd_hbm,
                         w_s, wup_s, wout_s, wg_s, wu_s, wd_s)

    @pl.when(t % tiles_per_seq == 0)
    def _():
        st_s[...] = jnp.zeros_like(st_s)
        u_s[0:SUBLANES, :] = jnp.zeros((SUBLANES, conv_w), jnp.float32)

    def make_streams():
        return {
            "m": _mixer_items(x_ref, g1_ref, w_s, wup_s, bg_ref, gn_ref, cw_ref, wout_s,
                              g2_ref, u_s, st_s, mix_s, x1_s, h2_s,
                              rows=rows, conv_w=conv_w, d_model=d_model),
            "f": _ffn_items(wg_s, wu_s, wd_s, gf_ref, out_ref, x1_s, xf_s, h2_s, a_s,
                            rows=rows, d_model=d_model, d_ff=d_ff, final_norm=final_norm),
        }

    @pl.when(t == 0)
    def _():
        for _ in make_streams()["m"]:
            pass

    @pl.when(t == n_tiles)
    def _():
        for _ in make_streams()["f"]:
            pass

    @pl.when(jnp.logical_and(t > 0, t < n_tiles))
    def _():
        streams = make_streams()
        for which in _ISSUE_ORDER:
            next(streams[which])
        for stream in streams.values():
            assert next(stream, None) is None


def _resident(shape):
    return pl.BlockSpec(shape, lambda *_: (0,) * len(shape), pipeline_mode=pl.Buffered(1))


def _block(x2d, seq, g1, w_in_t, w_up, b_gate, gn, cw, w_out, g2, wg, wu, wd, gf, final_norm):
    n_rows, d = x2d.shape
    rows = TILE_ROWS
    conv_w = cw.shape[1]
    d_ff = wg.shape[1]
    n_in = w_in_t.shape[0] - GATE_RANK + LANES
    assert seq % rows == 0 and rows % CHUNK == 0 and n_rows % seq == 0
    assert V_W + conv_w == w_out.shape[0] and conv_w % (2 * MXU_N) == 0
    assert w_in_t.shape[0] == 2 * QK_W + 2 * V_W + GATE_RANK + 3 * conv_w
    assert d % MXU_N == 0 and d_ff % MXU_N == 0 and GATE_RANK <= LANES
    n_tiles = n_rows // rows
    f32, bf16 = jnp.float32, jnp.bfloat16
    kern = functools.partial(_block_kernel, rows=rows, conv_w=conv_w, d_model=d, d_ff=d_ff,
                             tiles_per_seq=seq // rows, n_tiles=n_tiles, final_norm=final_norm)
    operands = (g1, w_in_t, w_up, b_gate, gn, cw, w_out, g2, wg, wu, wd, gf)
    in_hbm = {id(w) for w in (w_in_t, w_out, wg, wu, wd)}
    specs = [pl.BlockSpec(memory_space=pl.ANY) if id(w) in in_hbm else _resident(w.shape)
             for w in operands]
    return pl.pallas_call(
        kern,
        grid=(n_tiles + 1,),
        in_specs=[pl.BlockSpec((rows, d), lambda t: (jnp.minimum(t, n_tiles - 1), 0))] + specs,
        out_specs=pl.BlockSpec((rows, d), lambda t: (jnp.maximum(t - 1, 0), 0)),
        out_shape=jax.ShapeDtypeStruct(x2d.shape, x2d.dtype),
        scratch_shapes=[
            pltpu.VMEM((rows + SUBLANES, conv_w), f32),
            pltpu.VMEM((GLA_HEADS // 2, LANES, 2 * GLA_DV), f32),
            pltpu.VMEM((rows, V_W + conv_w), bf16),
            pltpu.VMEM((rows, d), f32),
            pltpu.VMEM((rows, d), f32),
            pltpu.VMEM((rows, d), bf16),
            pltpu.VMEM((rows, d_ff), bf16),
            pltpu.VMEM((d, n_in), bf16),
            pltpu.VMEM((LANES, QK_W), bf16),
            pltpu.VMEM(w_out.shape, bf16),
            pltpu.VMEM(wg.shape, bf16),
            pltpu.VMEM(wu.shape, bf16),
            pltpu.VMEM(wd.shape, bf16),
        ],
        compiler_params=pltpu.CompilerParams(
            dimension_semantics=("arbitrary",),
            vmem_limit_bytes=VMEM_LIMIT_BYTES),
        name="block",
    )(x2d, *operands)


def kernel(x, norm1_g, w_in, w_gate_up, b_gate, gla_norm_g, conv_w, w_out, norm2_g,
           w_ffn_gate, w_ffn_up, w_ffn_down, norm_f_g):
    bsz, seq, d = x.shape
    depth = w_in.shape[0]
    assert depth >= 1
    x = x.reshape(bsz * seq, d)
    for l in range(depth):
        w_in_t = jnp.swapaxes(w_in[l], 0, 1)
        x = _block(x, seq, norm1_g[l][None, :], w_in_t, w_gate_up[l], b_gate[l][None, :],
                   gla_norm_g[l][None, :], conv_w[l], w_out[l], norm2_g[l][None, :],
                   w_ffn_gate[l], w_ffn_up[l], w_ffn_down[l], norm_f_g[None, :],
                   final_norm=(l == depth - 1))
    return x.reshape(bsz, seq, d)
```

```python
import functools
import math

import jax
import jax.numpy as jnp
from jax import lax
from jax.experimental import pallas as pl
from jax.experimental.pallas import tpu as pltpu

GLA_HEADS = 4
GLA_DK = 64
GLA_DV = 128
GATE_RANK = 16
GATE_NORMALIZER = 16.0
CHUNK = 64
CONV_K = 3
NORM_EPS = 1e-6

LANES = 128
SUBLANES = 8
MXU_N = 256
QK_W = GLA_HEADS * GLA_DK
V_W = GLA_HEADS * GLA_DV

TILE_ROWS = 512
PREP_DMA_BYTES = 3 * 512 * 1024
PREP_SLOTS = 4
PREP_SLOTS_T = 8
LIVE_TILE_VALUES = 4
V7X_VMEM_BYTES = 64 * 1024 * 1024

_NT = (((1,), (1,)), ((), ()))
_TN = (((0,), (0,)), ((), ()))


def _dot(a, b):
    return jnp.dot(a, b, preferred_element_type=jnp.float32)


def _sigmoid(x):
    return 1.0 / (1.0 + jnp.exp(-x))


def _rms_scale(sum_sq, width):
    return lax.rsqrt(sum_sq * (1.0 / width) + NORM_EPS)


def _rms(x, g):
    y = x * _rms_scale(jnp.sum(x * x, axis=-1, keepdims=True), x.shape[-1])
    return y * g


def _mixer_items(x_ref, g1_ref, w_ref, wup_ref, bg_ref, gn_ref, cw_ref, wout_ref, g2_ref,
                 u_s, st_s, mix_s, x1_s, h2_s, *, rows, conv_w, d_model):
    f32, bf16 = jnp.float32, jnp.bfloat16
    c_q, c_v, c_g = 0, 2 * QK_W, 2 * QK_W + V_W
    c_cb = c_g + V_W
    c_cc, c_ch = c_cb + conv_w, c_cb + 2 * conv_w
    c_a = c_cb + 3 * conv_w
    pair_w = 2 * GLA_DV
    n_chunks = rows // CHUNK
    n_pairs = GLA_HEADS // 2
    chunk_rows = [slice(c * CHUNK, (c + 1) * CHUNK) for c in range(n_chunks)]
    pair_lanes = [slice(p * LANES, (p + 1) * LANES) for p in range(n_pairs)]
    pair_vlanes = [slice(p * pair_w, (p + 1) * pair_w) for p in range(n_pairs)]

    hb = _rms(x_ref[...], g1_ref[...]).astype(bf16)
    yield "norm1"

    pa = _dot(hb, w_ref[:, c_a:c_a + LANES])
    yield "gate_a"
    qk = _dot(hb, w_ref[:, c_q:c_v])
    yield "qk"
    gl = _dot(pa.astype(bf16), wup_ref[...]) + bg_ref[...]
    la = (jnp.minimum(gl, 0.0) - jnp.log1p(jnp.exp(-jnp.abs(gl)))) * (1.0 / GATE_NORMALIZER)
    la_hi = la.astype(bf16)
    la_lo = (la - la_hi.astype(f32)).astype(bf16)
    yield "gate"
    v = _dot(hb, w_ref[:, c_v:c_g]).astype(bf16)
    yield "v"
    pg = _dot(hb, w_ref[:, c_g:c_cb])
    yield "g"

    row = lax.broadcasted_iota(jnp.int32, (CHUNK, LANES), 0)
    lane = lax.broadcasted_iota(jnp.int32, (CHUNK, LANES), 1)
    tri2 = ((lane & (CHUNK - 1)) <= row).astype(bf16)
    head0_lanes = lane < GLA_DK
    row_q = lax.broadcasted_iota(jnp.int32, (CHUNK, QK_W), 0)
    lane_q = lax.broadcasted_iota(jnp.int32, (CHUNK, QK_W), 1)
    causal4 = (lane_q & (CHUNK - 1)) <= row_q
    head_lanes = [(lane_q >= hd * GLA_DK) & (lane_q < (hd + 1) * GLA_DK) for hd in range(GLA_HEADS)]
    zq = jnp.zeros((CHUNK, QK_W), bf16)
    lane_v = lax.broadcasted_iota(jnp.int32, (CHUNK, pair_w), 1)
    head0_vlanes = lane_v < GLA_DV
    zk = jnp.zeros((CHUNK, LANES), bf16)
    zv = jnp.zeros((CHUNK, pair_w), bf16)
    gn = gn_ref[...]

    half_w = conv_w // 2
    conv_dots = {}

    def conv_piece(name, half):
        base = {"cb": c_cb, "cc": c_cc, "ch": c_ch}[name] + half * half_w
        conv_dots[name, half] = _dot(hb, w_ref[:, base:base + half_w])

    def conv_finish(half):
        cs = slice(half * half_w, (half + 1) * half_w)
        u = conv_dots["cc", half] * conv_dots["ch", half]
        u_s[SUBLANES:SUBLANES + rows, cs] = u
        tap = [cw_ref[:, j * conv_w + cs.start:j * conv_w + cs.stop] for j in range(CONV_K)]
        y = tap[0] * u_s[SUBLANES - 2:SUBLANES - 2 + rows, cs]
        y = y + tap[1] * u_s[SUBLANES - 1:SUBLANES - 1 + rows, cs]
        y = y + tap[2] * u
        mix_s[:, V_W + half * half_w:V_W + (half + 1) * half_w] = (
            conv_dots["cb", half] * y).astype(bf16)
        u_s[0:SUBLANES, cs] = u_s[rows:rows + SUBLANES, cs]

    b_all = [_dot(tri2, jnp.concatenate([la_hi[rs], la_lo[rs]], axis=0)) for rs in chunk_rows]
    yield "gla_cumsum"
    conv_piece("cc", 0)
    yield "conv"

    q_i, k_bd, k_e_bd, v_bd, dec_col = {}, {}, {}, {}, {}
    for c, rs in enumerate(chunk_rows):
        b = b_all[c]
        b_last = b[CHUNK - 1:CHUNK, :]
        q_i[c] = (qk[rs, :QK_W] * (GLA_DK ** -0.5) * jnp.exp(b)).astype(bf16)
        k = qk[rs, QK_W:]
        k_i = (k * jnp.exp(-b)).astype(bf16)
        k_e = (k * jnp.exp(b_last - b)).astype(bf16)
        k_bd[c] = jnp.concatenate(
            [jnp.where(head_lanes[hd], k_i, zq) for hd in range(GLA_HEADS)], axis=0)
        b_tail = b[CHUNK - SUBLANES:CHUNK, :]
        for p in range(n_pairs):
            ls = pair_lanes[p]
            dec_col[c, p] = jnp.exp(b_tail[:, ls].T[:, SUBLANES - 1:SUBLANES])
            k_e_bd[c, p] = jnp.concatenate([jnp.where(head0_lanes, k_e[:, ls], zk),
                                            jnp.where(head0_lanes, zk, k_e[:, ls])], axis=0)
            v_p = v[rs, pair_vlanes[p]]
            v_bd[c, p] = jnp.concatenate([jnp.where(head0_vlanes, v_p, zv),
                                          jnp.where(head0_vlanes, zv, v_p)], axis=0)
    yield "gla_decay"
    conv_piece("ch", 0)
    yield "conv"

    sc, kv = {}, {}
    for c in range(n_chunks):
        sc[c] = lax.dot_general(q_i[c], k_bd[c], _NT, preferred_element_type=f32)
        for p in range(n_pairs):
            kv[c, p] = lax.dot_general(k_e_bd[c, p], v_bd[c, p], _TN, preferred_element_type=f32)
    yield "gla_scores"
    conv_piece("cb", 0)
    conv_finish(0)
    yield "conv"

    st = [st_s[p] for p in range(n_pairs)]
    st_prev = {}
    for c in range(n_chunks):
        sc[c] = jnp.where(causal4, sc[c], 0.0).astype(bf16)
        for p in range(n_pairs):
            st_prev[c, p] = st[p].astype(bf16)
            st[p] = st[p] * dec_col[c, p] + kv[c, p]
    for p in range(n_pairs):
        st_s[p] = st[p]
    yield "gla_state"
    conv_piece("cc", 1)
    yield "conv"

    o = {}
    for c in range(n_chunks):
        for p in range(n_pairs):
            ls = pair_lanes[p]
            lhs = jnp.concatenate([sc[c][:, ls], q_i[c][:, ls]], axis=1)
            rhs = jnp.concatenate([v_bd[c, p], st_prev[c, p]], axis=0)
            o[c, p] = _dot(lhs, rhs)
    yield "gla_out"
    conv_piece("ch", 1)
    yield "conv"

    for c, rs in enumerate(chunk_rows):
        for hd in range(GLA_HEADS):
            hs = slice((hd % 2) * GLA_DV, (hd % 2 + 1) * GLA_DV)
            g_h = pg[rs, hd * GLA_DV:(hd + 1) * GLA_DV]
            mix_s[rs, hd * GLA_DV:(hd + 1) * GLA_DV] = (
                _rms(o[c, hd // 2][:, hs], gn) * (g_h * _sigmoid(g_h))).astype(bf16)
    yield "gla_gate"
    conv_piece("cb", 1)
    conv_finish(1)
    yield "conv"

    sum_sq = jnp.zeros((rows, 1), f32)
    for n in range(d_model // MXU_N):
        cs = slice(n * MXU_N, (n + 1) * MXU_N)
        x1 = x_ref[:, cs] + _dot(mix_s[...], wout_ref[:, cs])
        x1_s[:, cs] = x1
        sum_sq = sum_sq + jnp.sum(x1 * x1, axis=-1, keepdims=True)
        yield "out_proj"
    h2_s[...] = (x1_s[...] * _rms_scale(sum_sq, d_model) * g2_ref[...]).astype(bf16)
    yield "norm2"


def _ffn_items(wg_ref, wu_ref, wd_ref, gf_ref, out_ref, x1_s, xf_s, h2_s, a_s,
               *, rows, d_model, d_ff, final_norm):
    f32, bf16 = jnp.float32, jnp.bfloat16
    for j in range(d_ff // MXU_N):
        if j == 2:
            xf_s[...] = x1_s[...]
            yield "copy"
        cs = slice(j * MXU_N, (j + 1) * MXU_N)
        h2 = h2_s[...]
        g = _dot(h2, wg_ref[:, cs])
        u = _dot(h2, wu_ref[:, cs])
        a_s[:, cs] = ((g * _sigmoid(g)) * u).astype(bf16)
        yield "gate_up"
    sum_sq = jnp.zeros((rows, 1), f32)
    for n in range(d_model // MXU_N):
        cs = slice(n * MXU_N, (n + 1) * MXU_N)
        y = xf_s[:, cs] + _dot(a_s[...], wd_ref[:, cs])
        out_ref[:, cs] = y
        sum_sq = sum_sq + jnp.sum(y * y, axis=-1, keepdims=True)
        yield "down"
    if final_norm:
        out_ref[...] = out_ref[...] * _rms_scale(sum_sq, d_model) * gf_ref[...]
    yield "final_norm"


_ISSUE_ORDER = (
    "f m "
    "f m m "
    "f f m m "
    "f m m "
    "f m m "
    "f m m "
    "f m m "
    "f m m "
    "f m m "
    "f m "
    "m f m "
    "f m f m "
    "m f f f"
).split()


def _stream_rows(src_hbm, consume):
    n_rows, width = src_hbm.shape
    rows = SUBLANES
    while 2 * rows * width * 4 <= PREP_DMA_BYTES and n_rows % (2 * rows) == 0:
        rows *= 2
    assert n_rows % rows == 0
    n = n_rows // rows
    ahead = PREP_SLOTS - 1

    def body(stage, sem):
        def copy(i, slot):
            return pltpu.make_async_copy(
                src_hbm.at[pl.ds(i * rows, rows), :], stage.at[slot], sem.at[slot])

        for i in range(min(ahead, n)):
            copy(i, i).start()

        def step(i, carry):
            slot = i % PREP_SLOTS
            copy(i, slot).wait()

            @pl.when(i + ahead < n)
            def _():
                copy(i + ahead, (i + ahead) % PREP_SLOTS).start()

            consume(pl.ds(pl.multiple_of(i * rows, rows), rows), stage[slot])
            return carry

        lax.fori_loop(0, n, step, 0)

    pl.run_scoped(body, pltpu.VMEM((PREP_SLOTS, rows, width), jnp.float32),
                  pltpu.SemaphoreType.DMA((PREP_SLOTS,)))


def _prepare_w_in(w_in_t_hbm, w_s):
    n_cols, d_model = w_in_t_hbm.shape
    a0 = 2 * QK_W + 2 * V_W
    a1 = a0 + GATE_RANK
    assert a0 % LANES == 0 and (n_cols - a1) % LANES == 0 and a0 + LANES <= n_cols
    blocks = [(r, r, LANES) for r in range(0, a0, LANES)]
    blocks += [(a1 + r, a0 + r, LANES) for r in range(0, n_cols - a1, LANES)]
    blocks += [(a0, a0 + n_cols - a1, GATE_RANK)]
    ahead = PREP_SLOTS_T - 1
    lane = lax.broadcasted_iota(jnp.int32, (d_model, LANES), 1)

    def body(stage, sem):
        def copy(j):
            slot = j % PREP_SLOTS_T
            return pltpu.make_async_copy(
                w_in_t_hbm.at[pl.ds(blocks[j][0], LANES), :], stage.at[slot], sem.at[slot])

        for j in range(min(ahead, len(blocks))):
            copy(j).start()
        for j, (_, dst, keep) in enumerate(blocks):
            copy(j).wait()
            if j + ahead < len(blocks):
                copy(j + ahead).start()
            blk = stage[j % PREP_SLOTS_T].T
            if keep < LANES:
                blk = jnp.where(lane < keep, blk, 0.0)
            w_s[:, dst:dst + LANES] = blk.astype(jnp.bfloat16)

    pl.run_scoped(body, pltpu.VMEM((PREP_SLOTS_T, LANES, d_model), jnp.float32),
                  pltpu.SemaphoreType.DMA((PREP_SLOTS_T,)))


def _prepare_weights(w_in_t_hbm, wup_ref, wout_hbm, wg_hbm, wu_hbm, wd_hbm,
                     w_s, wup_s, wout_s, wg_s, wu_s, wd_s):
    bf16 = jnp.bfloat16

    def put(dst):
        def consume(rs, blk):
            dst[rs, :] = blk.astype(bf16)
        return consume

    _prepare_w_in(w_in_t_hbm, w_s)
    _stream_rows(wout_hbm, put(wout_s))
    _stream_rows(wg_hbm, put(wg_s))
    _stream_rows(wu_hbm, put(wu_s))
    _stream_rows(wd_hbm, put(wd_s))
    wup_s[...] = jnp.zeros_like(wup_s)
    wup_s[0:GATE_RANK, :] = wup_ref[...].astype(bf16)


def _block_kernel(x_ref, g1_ref, w_in_t_hbm, wup_ref, bg_ref, gn_ref, cw_ref, wout_hbm,
                  g2_ref, wg_hbm, wu_hbm, wd_hbm, gf_ref, out_ref,
                  u_s, st_s, mix_s, x1_s, xf_s, h2_s, a_s,
                  w_s, wup_s, wout_s, wg_s, wu_s, wd_s,
                  *, rows, conv_w, d_model, d_ff, tiles_per_seq, final_norm):
    t = pl.program_id(0)

    @pl.when(t == 0)
    def _():
        _prepare_weights(w_in_t_hbm, wup_ref, wout_hbm, wg_hbm, wu_hbm, wd_hbm,
                         w_s, wup_s, wout_s, wg_s, wu_s, wd_s)
        x1_s[...] = jnp.zeros_like(x1_s)
        h2_s[...] = jnp.zeros_like(h2_s)

    @pl.when(t % tiles_per_seq == 0)
    def _():
        st_s[...] = jnp.zeros_like(st_s)
        u_s[0:SUBLANES, :] = jnp.zeros((SUBLANES, conv_w), jnp.float32)

    streams = {
        "m": _mixer_items(x_ref, g1_ref, w_s, wup_s, bg_ref, gn_ref, cw_ref, wout_s,
                          g2_ref, u_s, st_s, mix_s, x1_s, h2_s,
                          rows=rows, conv_w=conv_w, d_model=d_model),
        "f": _ffn_items(wg_s, wu_s, wd_s, gf_ref, out_ref, x1_s, xf_s, h2_s, a_s,
                        rows=rows, d_model=d_model, d_ff=d_ff, final_norm=final_norm),
    }
    for which in _ISSUE_ORDER:
        next(streams[which])
    for stream in streams.values():
        assert next(stream, None) is None


def _resident(shape):
    return pl.BlockSpec(shape, lambda *_: (0,) * len(shape), pipeline_mode=pl.Buffered(1))


def _block(x2d, seq, g1, w_in_t, w_up, b_gate, gn, cw, w_out, g2, wg, wu, wd, gf, final_norm):
    n_rows, d = x2d.shape
    rows = TILE_ROWS
    conv_w = cw.shape[1] // CONV_K
    d_ff = wg.shape[1]
    n_in = w_in_t.shape[0] - GATE_RANK + LANES
    assert seq % rows == 0 and rows % CHUNK == 0 and n_rows % seq == 0
    assert V_W + conv_w == w_out.shape[0] and conv_w % (2 * MXU_N) == 0
    assert w_in_t.shape[0] == 2 * QK_W + 2 * V_W + GATE_RANK + 3 * conv_w
    assert d % MXU_N == 0 and d_ff % MXU_N == 0 and GATE_RANK <= LANES
    n_tiles = n_rows // rows
    f32, bf16 = jnp.float32, jnp.bfloat16
    kern = functools.partial(_block_kernel, rows=rows, conv_w=conv_w, d_model=d, d_ff=d_ff,
                             tiles_per_seq=seq // rows, final_norm=final_norm)
    operands = (g1, w_in_t, w_up, b_gate, gn, cw, w_out, g2, wg, wu, wd, gf)
    in_hbm = {id(w) for w in (w_in_t, w_out, wg, wu, wd)}
    specs = [pl.BlockSpec(memory_space=pl.ANY) if id(w) in in_hbm else _resident(w.shape)
             for w in operands]
    scratch = [
        ((rows + SUBLANES, conv_w), f32),
        ((GLA_HEADS // 2, LANES, 2 * GLA_DV), f32),
        ((rows, V_W + conv_w), bf16),
        ((rows, d), f32),
        ((rows, d), f32),
        ((rows, d), bf16),
        ((rows, d_ff), bf16),
        ((d, n_in), bf16),
        ((LANES, QK_W), bf16),
        (w_out.shape, bf16),
        (wg.shape, bf16),
        (wu.shape, bf16),
        (wd.shape, bf16),
    ]

    def nbytes(shape, dtype):
        return math.prod(shape) * jnp.dtype(dtype).itemsize

    vmem_bytes = (sum(nbytes(*sd) for sd in scratch) + 2 * 2 * nbytes((rows, d), f32)
                  + max(PREP_SLOTS * PREP_DMA_BYTES, PREP_SLOTS_T * nbytes((LANES, d), f32))
                  + LIVE_TILE_VALUES * nbytes((rows, d), f32))
    assert vmem_bytes <= V7X_VMEM_BYTES
    return pl.pallas_call(
        kern,
        grid=(n_tiles + 1,),
        in_specs=[pl.BlockSpec((rows, d), lambda t: (jnp.minimum(t, n_tiles - 1), 0))] + specs,
        out_specs=pl.BlockSpec((rows, d), lambda t: (jnp.maximum(t - 1, 0), 0)),
        out_shape=jax.ShapeDtypeStruct(x2d.shape, x2d.dtype),
        scratch_shapes=[pltpu.VMEM(shape, dtype) for shape, dtype in scratch],
        compiler_params=pltpu.CompilerParams(
            dimension_semantics=("arbitrary",),
            vmem_limit_bytes=vmem_bytes),
        name="block",
    )(x2d, *operands)


def kernel(x, norm1_g, w_in, w_gate_up, b_gate, gla_norm_g, conv_w, w_out, norm2_g,
           w_ffn_gate, w_ffn_up, w_ffn_down, norm_f_g):
    bsz, seq, d = x.shape
    depth = w_in.shape[0]
    assert depth >= 1 and conv_w.shape[1] == CONV_K
    x = x.reshape(bsz * seq, d)
    for l in range(depth):
        w_in_t = jnp.swapaxes(w_in[l], 0, 1)
        x = _block(x, seq, norm1_g[l][None, :], w_in_t, w_gate_up[l], b_gate[l][None, :],
                   gla_norm_g[l][None, :], conv_w[l].reshape(1, -1), w_out[l], norm2_g[l][None, :],
                   w_ffn_gate[l], w_ffn_up[l], w_ffn_down[l], norm_f_g[None, :],
                   final_norm=(l == depth - 1))
    return x.reshape(bsz, seq, d)
```

```python
import functools
import math

import jax
import jax.numpy as jnp
from jax import lax
from jax.experimental import pallas as pl
from jax.experimental.pallas import tpu as pltpu

GLA_HEADS = 4
GLA_DK = 64
GLA_DV = 128
GATE_RANK = 16
GATE_NORMALIZER = 16.0
CHUNK = 64
CONV_K = 3
NORM_EPS = 1e-6

LANES = 128
SUBLANES = 8
MXU_N = 256
QK_W = GLA_HEADS * GLA_DK
V_W = GLA_HEADS * GLA_DV

TILE_ROWS = 512
PREP_DMA_BYTES = 3 * 512 * 1024
PREP_SLOTS = 4
PREP_SLOTS_T = 8
LIVE_TILE_VALUES = 3
V7X_VMEM_BYTES = 64 * 1024 * 1024

_NT = (((1,), (1,)), ((), ()))
_TN = (((0,), (0,)), ((), ()))


def _dot(a, b):
    return jnp.dot(a, b, preferred_element_type=jnp.float32)


def _sigmoid(x):
    return 1.0 / (1.0 + jnp.exp(-x))


def _rms_scale(sum_sq, width):
    return lax.rsqrt(sum_sq * (1.0 / width) + NORM_EPS)


def _rms(x, g):
    y = x * _rms_scale(jnp.sum(x * x, axis=-1, keepdims=True), x.shape[-1])
    return y * g


def _mixer_items(x_ref, g1_ref, w_ref, wup_ref, bg_ref, gn_ref, cw_ref, wout_ref, g2_ref,
                 u_s, st_s, mix_s, x1_s, h2_s, *, rows, conv_w, d_model):
    f32, bf16 = jnp.float32, jnp.bfloat16
    c_q, c_v, c_g = 0, 2 * QK_W, 2 * QK_W + V_W
    c_cb = c_g + V_W
    c_cc, c_ch = c_cb + conv_w, c_cb + 2 * conv_w
    c_a = c_cb + 3 * conv_w
    pair_w = 2 * GLA_DV
    n_chunks = rows // CHUNK
    n_pairs = GLA_HEADS // 2
    chunk_rows = [slice(c * CHUNK, (c + 1) * CHUNK) for c in range(n_chunks)]
    pair_lanes = [slice(p * LANES, (p + 1) * LANES) for p in range(n_pairs)]
    pair_vlanes = [slice(p * pair_w, (p + 1) * pair_w) for p in range(n_pairs)]

    hb = _rms(x_ref[...], g1_ref[...]).astype(bf16)
    yield "norm1"

    pa = _dot(hb, w_ref[:, c_a:c_a + LANES])
    yield "gate_a"
    qk = _dot(hb, w_ref[:, c_q:c_v])
    yield "qk"
    gl = _dot(pa.astype(bf16), wup_ref[...]) + bg_ref[...]
    la = (jnp.minimum(gl, 0.0) - jnp.log1p(jnp.exp(-jnp.abs(gl)))) * (1.0 / GATE_NORMALIZER)
    la_hi = la.astype(bf16)
    la_lo = (la - la_hi.astype(f32)).astype(bf16)
    yield "gate"
    v = _dot(hb, w_ref[:, c_v:c_g]).astype(bf16)
    yield "v"
    pg = _dot(hb, w_ref[:, c_g:c_cb])
    yield "g"

    row = lax.broadcasted_iota(jnp.int32, (CHUNK, LANES), 0)
    lane = lax.broadcasted_iota(jnp.int32, (CHUNK, LANES), 1)
    tri2 = ((lane & (CHUNK - 1)) <= row).astype(bf16)
    head0_lanes = lane < GLA_DK
    row_q = lax.broadcasted_iota(jnp.int32, (CHUNK, QK_W), 0)
    lane_q = lax.broadcasted_iota(jnp.int32, (CHUNK, QK_W), 1)
    causal4 = (lane_q & (CHUNK - 1)) <= row_q
    head_lanes = [(lane_q >= hd * GLA_DK) & (lane_q < (hd + 1) * GLA_DK) for hd in range(GLA_HEADS)]
    zq = jnp.zeros((CHUNK, QK_W), bf16)
    lane_v = lax.broadcasted_iota(jnp.int32, (CHUNK, pair_w), 1)
    head0_vlanes = lane_v < GLA_DV
    zk = jnp.zeros((CHUNK, LANES), bf16)
    zv = jnp.zeros((CHUNK, pair_w), bf16)
    gn = gn_ref[...]

    half_w = conv_w // 2
    conv_dots = {}

    def conv_piece(name, half):
        base = {"cb": c_cb, "cc": c_cc, "ch": c_ch}[name] + half * half_w
        conv_dots[name, half] = _dot(hb, w_ref[:, base:base + half_w])

    def conv_finish(half):
        cs = slice(half * half_w, (half + 1) * half_w)
        u = conv_dots["cc", half] * conv_dots["ch", half]
        u_s[SUBLANES:SUBLANES + rows, cs] = u
        tap = [cw_ref[:, j * conv_w + cs.start:j * conv_w + cs.stop] for j in range(CONV_K)]
        y = tap[0] * u_s[SUBLANES - 2:SUBLANES - 2 + rows, cs]
        y = y + tap[1] * u_s[SUBLANES - 1:SUBLANES - 1 + rows, cs]
        y = y + tap[2] * u
        mix_s[:, V_W + half * half_w:V_W + (half + 1) * half_w] = (
            conv_dots["cb", half] * y).astype(bf16)
        u_s[0:SUBLANES, cs] = u_s[rows:rows + SUBLANES, cs]

    b_all = [_dot(tri2, jnp.concatenate([la_hi[rs], la_lo[rs]], axis=0)) for rs in chunk_rows]
    yield "gla_cumsum"
    conv_piece("cc", 0)
    yield "conv"

    q_i, k_bd, k_e_bd, v_bd, dec_col = {}, {}, {}, {}, {}
    for c, rs in enumerate(chunk_rows):
        b = b_all[c]
        b_last = b[CHUNK - 1:CHUNK, :]
        q_i[c] = (qk[rs, :QK_W] * (GLA_DK ** -0.5) * jnp.exp(b)).astype(bf16)
        k = qk[rs, QK_W:]
        k_i = (k * jnp.exp(-b)).astype(bf16)
        k_e = (k * jnp.exp(b_last - b)).astype(bf16)
        k_bd[c] = jnp.concatenate(
            [jnp.where(head_lanes[hd], k_i, zq) for hd in range(GLA_HEADS)], axis=0)
        b_tail = b[CHUNK - SUBLANES:CHUNK, :]
        for p in range(n_pairs):
            ls = pair_lanes[p]
            dec_col[c, p] = jnp.exp(b_tail[:, ls].T[:, SUBLANES - 1:SUBLANES])
            k_e_bd[c, p] = jnp.concatenate([jnp.where(head0_lanes, k_e[:, ls], zk),
                                            jnp.where(head0_lanes, zk, k_e[:, ls])], axis=0)
            v_p = v[rs, pair_vlanes[p]]
            v_bd[c, p] = jnp.concatenate([jnp.where(head0_vlanes, v_p, zv),
                                          jnp.where(head0_vlanes, zv, v_p)], axis=0)
    yield "gla_decay"
    conv_piece("ch", 0)
    yield "conv"

    sc, kv = {}, {}
    for c in range(n_chunks):
        sc[c] = lax.dot_general(q_i[c], k_bd[c], _NT, preferred_element_type=f32)
        for p in range(n_pairs):
            kv[c, p] = lax.dot_general(k_e_bd[c, p], v_bd[c, p], _TN, preferred_element_type=f32)
    yield "gla_scores"
    conv_piece("cb", 0)
    conv_finish(0)
    yield "conv"

    st = [st_s[p] for p in range(n_pairs)]
    st_prev = {}
    for c in range(n_chunks):
        sc[c] = jnp.where(causal4, sc[c], 0.0).astype(bf16)
        for p in range(n_pairs):
            st_prev[c, p] = st[p].astype(bf16)
            st[p] = st[p] * dec_col[c, p] + kv[c, p]
    for p in range(n_pairs):
        st_s[p] = st[p]
    yield "gla_state"
    conv_piece("cc", 1)
    yield "conv"

    o = {}
    for c in range(n_chunks):
        for p in range(n_pairs):
            ls = pair_lanes[p]
            lhs = jnp.concatenate([sc[c][:, ls], q_i[c][:, ls]], axis=1)
            rhs = jnp.concatenate([v_bd[c, p], st_prev[c, p]], axis=0)
            o[c, p] = _dot(lhs, rhs)
    yield "gla_out"
    conv_piece("ch", 1)
    yield "conv"

    for c, rs in enumerate(chunk_rows):
        for hd in range(GLA_HEADS):
            hs = slice((hd % 2) * GLA_DV, (hd % 2 + 1) * GLA_DV)
            g_h = pg[rs, hd * GLA_DV:(hd + 1) * GLA_DV]
            mix_s[rs, hd * GLA_DV:(hd + 1) * GLA_DV] = (
                _rms(o[c, hd // 2][:, hs], gn) * (g_h * _sigmoid(g_h))).astype(bf16)
    yield "gla_gate"
    conv_piece("cb", 1)
    conv_finish(1)
    yield "conv"

    sum_sq = jnp.zeros((rows, 1), f32)
    for n in range(d_model // MXU_N):
        cs = slice(n * MXU_N, (n + 1) * MXU_N)
        x1 = x_ref[:, cs] + _dot(mix_s[...], wout_ref[:, cs])
        x1_s[:, cs] = x1
        sum_sq = sum_sq + jnp.sum(x1 * x1, axis=-1, keepdims=True)
        yield "out_proj"
    h2_s[...] = (x1_s[...] * _rms_scale(sum_sq, d_model) * g2_ref[...]).astype(bf16)
    yield "norm2"


def _ffn_items(wg_ref, wu_ref, wd_ref, gf_ref, out_ref, x1_s, xf_s, h2_s, a_s,
               *, rows, d_model, d_ff, final_norm):
    f32, bf16 = jnp.float32, jnp.bfloat16
    for j in range(d_ff // MXU_N):
        if j == 2:
            xf_s[...] = x1_s[...]
            yield "copy"
        cs = slice(j * MXU_N, (j + 1) * MXU_N)
        h2 = h2_s[...]
        g = _dot(h2, wg_ref[:, cs])
        u = _dot(h2, wu_ref[:, cs])
        a_s[:, cs] = ((g * _sigmoid(g)) * u).astype(bf16)
        yield "gate_up"
    sum_sq = jnp.zeros((rows, 1), f32)
    for n in range(d_model // MXU_N):
        cs = slice(n * MXU_N, (n + 1) * MXU_N)
        y = xf_s[:, cs] + _dot(a_s[...], wd_ref[:, cs])
        out_ref[:, cs] = y
        sum_sq = sum_sq + jnp.sum(y * y, axis=-1, keepdims=True)
        yield "down"
    if final_norm:
        out_ref[...] = out_ref[...] * _rms_scale(sum_sq, d_model) * gf_ref[...]
    yield "final_norm"


_ISSUE_ORDER = (
    "f m "
    "f m m "
    "f f m m "
    "f m m "
    "f m m "
    "f m m "
    "f m m "
    "f m m "
    "f m m "
    "f m "
    "m f m "
    "f m f m "
    "m f f f"
).split()


def _prep_rows(shape):
    n_rows, width = shape
    rows = SUBLANES
    while 2 * rows * width * 4 <= PREP_DMA_BYTES and n_rows % (2 * rows) == 0:
        rows *= 2
    return rows


def _stream_rows(src_hbm, consume):
    n_rows, width = src_hbm.shape
    rows = _prep_rows(src_hbm.shape)
    assert n_rows % rows == 0
    n = n_rows // rows
    ahead = PREP_SLOTS - 1

    def body(stage, sem):
        def copy(i, slot):
            return pltpu.make_async_copy(
                src_hbm.at[pl.ds(i * rows, rows), :], stage.at[slot], sem.at[slot])

        for i in range(min(ahead, n)):
            copy(i, i).start()

        def step(i, carry):
            slot = i % PREP_SLOTS
            copy(i, slot).wait()

            @pl.when(i + ahead < n)
            def _():
                copy(i + ahead, (i + ahead) % PREP_SLOTS).start()

            consume(pl.ds(pl.multiple_of(i * rows, rows), rows), stage[slot])
            return carry

        lax.fori_loop(0, n, step, 0)

    pl.run_scoped(body, pltpu.VMEM((PREP_SLOTS, rows, width), jnp.float32),
                  pltpu.SemaphoreType.DMA((PREP_SLOTS,)))


def _prepare_w_in(w_in_t_hbm, w_s):
    n_cols, d_model = w_in_t_hbm.shape
    a0 = 2 * QK_W + 2 * V_W
    a1 = a0 + GATE_RANK
    assert a0 % LANES == 0 and (n_cols - a1) % LANES == 0 and a0 + LANES <= n_cols
    blocks = [(r, r, LANES) for r in range(0, a0, LANES)]
    blocks += [(a1 + r, a0 + r, LANES) for r in range(0, n_cols - a1, LANES)]
    blocks += [(a0, a0 + n_cols - a1, GATE_RANK)]
    ahead = PREP_SLOTS_T - 1
    lane = lax.broadcasted_iota(jnp.int32, (d_model, LANES), 1)

    def body(stage, sem):
        def copy(j):
            slot = j % PREP_SLOTS_T
            return pltpu.make_async_copy(
                w_in_t_hbm.at[pl.ds(blocks[j][0], LANES), :], stage.at[slot], sem.at[slot])

        for j in range(min(ahead, len(blocks))):
            copy(j).start()
        for j, (_, dst, keep) in enumerate(blocks):
            copy(j).wait()
            if j + ahead < len(blocks):
                copy(j + ahead).start()
            blk = stage[j % PREP_SLOTS_T].T
            if keep < LANES:
                blk = jnp.where(lane < keep, blk, 0.0)
            w_s[:, dst:dst + LANES] = blk.astype(jnp.bfloat16)

    pl.run_scoped(body, pltpu.VMEM((PREP_SLOTS_T, LANES, d_model), jnp.float32),
                  pltpu.SemaphoreType.DMA((PREP_SLOTS_T,)))


def _prepare_mixer_weights(w_in_t_hbm, wup_ref, wout_hbm, w_s, wup_s, wout_s):
    bf16 = jnp.bfloat16

    def put_out(rs, blk):
        wout_s[rs, :] = blk.astype(bf16)

    _prepare_w_in(w_in_t_hbm, w_s)
    _stream_rows(wout_hbm, put_out)
    wup_s[...] = jnp.zeros_like(wup_s)
    wup_s[0:GATE_RANK, :] = wup_ref[...].astype(bf16)


def _ffn_weight_items(weights, rings):
    ahead = PREP_SLOTS - 1
    by_ring = {}
    for (src, dst), ring in zip(weights, rings):
        rows = _prep_rows(src.shape)
        assert src.shape[0] % rows == 0
        by_ring.setdefault(id(ring[0]), (ring, []))[1].extend(
            (src, dst, r, rows) for r in range(0, src.shape[0], rows))

    def copy(ring, blocks, j):
        stage, sem = ring
        src, _, r, rows = blocks[j]
        slot = j % PREP_SLOTS
        return pltpu.make_async_copy(src.at[pl.ds(r, rows), :], stage.at[slot], sem.at[slot])

    for ring, blocks in by_ring.values():
        for j in range(min(ahead, len(blocks))):
            copy(ring, blocks, j).start()
    for ring, blocks in by_ring.values():
        for j, (_, dst, r, rows) in enumerate(blocks):
            copy(ring, blocks, j).wait()
            if j + ahead < len(blocks):
                copy(ring, blocks, j + ahead).start()
            dst[r:r + rows, :] = ring[0][j % PREP_SLOTS].astype(jnp.bfloat16)
            yield "cast"


def _block_kernel(x_ref, g1_ref, w_in_t_hbm, wup_ref, bg_ref, gn_ref, cw_ref, wout_hbm,
                  g2_ref, wg_hbm, wu_hbm, wd_hbm, gf_ref, out_ref,
                  u_s, st_s, mix_s, x1_s, xf_s, h2_s, a_s,
                  w_s, wup_s, wout_s, wg_s, wu_s, wd_s,
                  *, rows, conv_w, d_model, d_ff, tiles_per_seq, final_norm):
    t = pl.program_id(0)

    @pl.when(t % tiles_per_seq == 0)
    def _():
        st_s[...] = jnp.zeros_like(st_s)
        u_s[0:SUBLANES, :] = jnp.zeros((SUBLANES, conv_w), jnp.float32)

    def mixer_items():
        return _mixer_items(x_ref, g1_ref, w_s, wup_s, bg_ref, gn_ref, cw_ref, wout_s,
                            g2_ref, u_s, st_s, mix_s, x1_s, h2_s,
                            rows=rows, conv_w=conv_w, d_model=d_model)

    @pl.when(t == 0)
    def _():
        _prepare_mixer_weights(w_in_t_hbm, wup_ref, wout_hbm, w_s, wup_s, wout_s)

        def body(stage_gu, sem_gu, stage_d, sem_d):
            ring_gu, ring_d = (stage_gu, sem_gu), (stage_d, sem_d)
            casts = _ffn_weight_items(((wg_hbm, wg_s), (wu_hbm, wu_s), (wd_hbm, wd_s)),
                                      (ring_gu, ring_gu, ring_d))
            mixer = mixer_items()
            while True:
                done = [next(casts, None) is None, next(mixer, None) is None]
                if all(done):
                    break

        pl.run_scoped(
            body,
            pltpu.VMEM((PREP_SLOTS, _prep_rows(wg_hbm.shape)) + wg_hbm.shape[1:], jnp.float32),
            pltpu.SemaphoreType.DMA((PREP_SLOTS,)),
            pltpu.VMEM((PREP_SLOTS, _prep_rows(wd_hbm.shape)) + wd_hbm.shape[1:], jnp.float32),
            pltpu.SemaphoreType.DMA((PREP_SLOTS,)))

    @pl.when(t > 0)
    def _():
        streams = {
            "m": mixer_items(),
            "f": _ffn_items(wg_s, wu_s, wd_s, gf_ref, out_ref, x1_s, xf_s, h2_s, a_s,
                            rows=rows, d_model=d_model, d_ff=d_ff, final_norm=final_norm),
        }
        for which in _ISSUE_ORDER:
            next(streams[which])
        for stream in streams.values():
            assert next(stream, None) is None


def _resident(shape):
    return pl.BlockSpec(shape, lambda *_: (0,) * len(shape), pipeline_mode=pl.Buffered(1))


def _block(x2d, seq, g1, w_in_t, w_up, b_gate, gn, cw, w_out, g2, wg, wu, wd, gf, final_norm):
    n_rows, d = x2d.shape
    rows = TILE_ROWS
    conv_w = cw.shape[1] // CONV_K
    d_ff = wg.shape[1]
    n_in = w_in_t.shape[0] - GATE_RANK + LANES
    assert seq % rows == 0 and rows % CHUNK == 0 and n_rows % seq == 0
    assert V_W + conv_w == w_out.shape[0] and conv_w % (2 * MXU_N) == 0
    assert w_in_t.shape[0] == 2 * QK_W + 2 * V_W + GATE_RANK + 3 * conv_w
    assert d % MXU_N == 0 and d_ff % MXU_N == 0 and GATE_RANK <= LANES and wg.shape == wu.shape
    n_tiles = n_rows // rows
    f32, bf16 = jnp.float32, jnp.bfloat16
    kern = functools.partial(_block_kernel, rows=rows, conv_w=conv_w, d_model=d, d_ff=d_ff,
                             tiles_per_seq=seq // rows, final_norm=final_norm)
    operands = (g1, w_in_t, w_up, b_gate, gn, cw, w_out, g2, wg, wu, wd, gf)
    in_hbm = {id(w) for w in (w_in_t, w_out, wg, wu, wd)}
    specs = [pl.BlockSpec(memory_space=pl.ANY) if id(w) in in_hbm else _resident(w.shape)
             for w in operands]
    scratch = [
        ((rows + SUBLANES, conv_w), f32),
        ((GLA_HEADS // 2, LANES, 2 * GLA_DV), f32),
        ((rows, V_W + conv_w), bf16),
        ((rows, d), f32),
        ((rows, d), f32),
        ((rows, d), bf16),
        ((rows, d_ff), bf16),
        ((d, n_in), bf16),
        ((LANES, QK_W), bf16),
        (w_out.shape, bf16),
        (wg.shape, bf16),
        (wu.shape, bf16),
        (wd.shape, bf16),
    ]

    def nbytes(shape, dtype):
        return math.prod(shape) * jnp.dtype(dtype).itemsize

    def ring_bytes(shape, slots):
        return slots * nbytes((_prep_rows(shape),) + tuple(shape[1:]), f32)

    staging = max(PREP_SLOTS_T * nbytes((LANES, d), f32), ring_bytes(w_out.shape, PREP_SLOTS),
                  ring_bytes(wg.shape, PREP_SLOTS) + ring_bytes(wd.shape, PREP_SLOTS))
    vmem_bytes = (sum(nbytes(*sd) for sd in scratch) + 2 * 2 * nbytes((rows, d), f32)
                  + staging + LIVE_TILE_VALUES * nbytes((rows, d), f32))
    assert vmem_bytes <= V7X_VMEM_BYTES
    return pl.pallas_call(
        kern,
        grid=(n_tiles + 1,),
        in_specs=[pl.BlockSpec((rows, d), lambda t: (jnp.minimum(t, n_tiles - 1), 0))] + specs,
        out_specs=pl.BlockSpec((rows, d), lambda t: (jnp.maximum(t - 1, 0), 0)),
        out_shape=jax.ShapeDtypeStruct(x2d.shape, x2d.dtype),
        scratch_shapes=[pltpu.VMEM(shape, dtype) for shape, dtype in scratch],
        compiler_params=pltpu.CompilerParams(
            dimension_semantics=("arbitrary",),
            vmem_limit_bytes=vmem_bytes),
        name="block",
    )(x2d, *operands)


def kernel(x, norm1_g, w_in, w_gate_up, b_gate, gla_norm_g, conv_w, w_out, norm2_g,
           w_ffn_gate, w_ffn_up, w_ffn_down, norm_f_g):
    bsz, seq, d = x.shape
    depth = w_in.shape[0]
    assert depth >= 1 and conv_w.shape[1] == CONV_K
    x = x.reshape(bsz * seq, d)
    for l in range(depth):
        w_in_t = jnp.swapaxes(w_in[l], 0, 1)
        x = _block(x, seq, norm1_g[l][None, :], w_in_t, w_gate_up[l], b_gate[l][None, :],
                   gla_norm_g[l][None, :], conv_w[l].reshape(1, -1), w_out[l], norm2_g[l][None, :],
                   w_ffn_gate[l], w_ffn_up[l], w_ffn_down[l], norm_f_g[None, :],
                   final_norm=(l == depth - 1))
    return x.reshape(bsz, seq, d)
```

```python
import functools
import math

import jax
import jax.numpy as jnp
from jax import lax
from jax.experimental import pallas as pl
from jax.experimental.pallas import tpu as pltpu

GLA_HEADS = 4
GLA_DK = 64
GLA_DV = 128
GATE_RANK = 16
GATE_NORMALIZER = 16.0
CHUNK = 64
CONV_K = 3
NORM_EPS = 1e-6

LANES = 128
SUBLANES = 8
MXU_N = 256
QK_W = GLA_HEADS * GLA_DK
V_W = GLA_HEADS * GLA_DV

TILE_ROWS = 512
PREP_DMA_BYTES = 3 * 512 * 1024
PREP_SLOTS = 4
CAST_BLOCKS_PER_ITEM = 3
PREP_SLOTS_T = 8
LIVE_TILE_VALUES = 3
V7X_VMEM_BYTES = 64 * 1024 * 1024

_NT = (((1,), (1,)), ((), ()))
_TN = (((0,), (0,)), ((), ()))


def _dot(a, b):
    return jnp.dot(a, b, preferred_element_type=jnp.float32)


def _sigmoid(x):
    return 1.0 / (1.0 + jnp.exp(-x))


def _rms_scale(sum_sq, width):
    return lax.rsqrt(sum_sq * (1.0 / width) + NORM_EPS)


def _rms(x, g):
    y = x * _rms_scale(jnp.sum(x * x, axis=-1, keepdims=True), x.shape[-1])
    return y * g


def _mixer_items(x_ref, g1_ref, w_ref, wup_ref, bg_ref, gn_ref, cw_ref, wout_ref, g2_ref,
                 u_s, st_s, mix_s, x1_s, h2_s, *, rows, conv_w, d_model):
    f32, bf16 = jnp.float32, jnp.bfloat16
    c_q, c_v, c_g = 0, 2 * QK_W, 2 * QK_W + V_W
    c_cb = c_g + V_W
    c_cc, c_ch = c_cb + conv_w, c_cb + 2 * conv_w
    c_a = c_cb + 3 * conv_w
    pair_w = 2 * GLA_DV
    n_chunks = rows // CHUNK
    n_pairs = GLA_HEADS // 2
    chunk_rows = [slice(c * CHUNK, (c + 1) * CHUNK) for c in range(n_chunks)]
    pair_lanes = [slice(p * LANES, (p + 1) * LANES) for p in range(n_pairs)]
    pair_vlanes = [slice(p * pair_w, (p + 1) * pair_w) for p in range(n_pairs)]

    hb = _rms(x_ref[...], g1_ref[...]).astype(bf16)
    yield "norm1"

    pa = _dot(hb, w_ref[:, c_a:c_a + LANES])
    yield "gate_a"
    qk = _dot(hb, w_ref[:, c_q:c_v])
    yield "qk"
    gl = _dot(pa.astype(bf16), wup_ref[...]) + bg_ref[...]
    la = (jnp.minimum(gl, 0.0) - jnp.log1p(jnp.exp(-jnp.abs(gl)))) * (1.0 / GATE_NORMALIZER)
    la_hi = la.astype(bf16)
    la_lo = (la - la_hi.astype(f32)).astype(bf16)
    yield "gate"
    v = _dot(hb, w_ref[:, c_v:c_g]).astype(bf16)
    yield "v"
    pg = _dot(hb, w_ref[:, c_g:c_cb])
    yield "g"

    row = lax.broadcasted_iota(jnp.int32, (CHUNK, LANES), 0)
    lane = lax.broadcasted_iota(jnp.int32, (CHUNK, LANES), 1)
    tri2 = ((lane & (CHUNK - 1)) <= row).astype(bf16)
    head0_lanes = lane < GLA_DK
    row_q = lax.broadcasted_iota(jnp.int32, (CHUNK, QK_W), 0)
    lane_q = lax.broadcasted_iota(jnp.int32, (CHUNK, QK_W), 1)
    causal4 = (lane_q & (CHUNK - 1)) <= row_q
    head_lanes = [(lane_q >= hd * GLA_DK) & (lane_q < (hd + 1) * GLA_DK) for hd in range(GLA_HEADS)]
    zq = jnp.zeros((CHUNK, QK_W), bf16)
    lane_v = lax.broadcasted_iota(jnp.int32, (CHUNK, pair_w), 1)
    head0_vlanes = lane_v < GLA_DV
    zk = jnp.zeros((CHUNK, LANES), bf16)
    zv = jnp.zeros((CHUNK, pair_w), bf16)
    gn = gn_ref[...]

    half_w = conv_w // 2
    conv_dots = {}

    def conv_piece(name, half):
        base = {"cb": c_cb, "cc": c_cc, "ch": c_ch}[name] + half * half_w
        conv_dots[name, half] = _dot(hb, w_ref[:, base:base + half_w])

    def conv_finish(half):
        cs = slice(half * half_w, (half + 1) * half_w)
        u = conv_dots["cc", half] * conv_dots["ch", half]
        u_s[SUBLANES:SUBLANES + rows, cs] = u
        tap = [cw_ref[:, j * conv_w + cs.start:j * conv_w + cs.stop] for j in range(CONV_K)]
        y = tap[0] * u_s[SUBLANES - 2:SUBLANES - 2 + rows, cs]
        y = y + tap[1] * u_s[SUBLANES - 1:SUBLANES - 1 + rows, cs]
        y = y + tap[2] * u
        mix_s[:, V_W + half * half_w:V_W + (half + 1) * half_w] = (
            conv_dots["cb", half] * y).astype(bf16)
        u_s[0:SUBLANES, cs] = u_s[rows:rows + SUBLANES, cs]

    b_all = [_dot(tri2, jnp.concatenate([la_hi[rs], la_lo[rs]], axis=0)) for rs in chunk_rows]
    yield "gla_cumsum"
    conv_piece("cc", 0)
    yield "conv"

    q_i, k_bd, k_e_bd, v_bd, dec_col = {}, {}, {}, {}, {}
    for c, rs in enumerate(chunk_rows):
        b = b_all[c]
        b_last = b[CHUNK - 1:CHUNK, :]
        q_i[c] = (qk[rs, :QK_W] * (GLA_DK ** -0.5) * jnp.exp(b)).astype(bf16)
        k = qk[rs, QK_W:]
        k_i = (k * jnp.exp(-b)).astype(bf16)
        k_e = (k * jnp.exp(b_last - b)).astype(bf16)
        k_bd[c] = jnp.concatenate(
            [jnp.where(head_lanes[hd], k_i, zq) for hd in range(GLA_HEADS)], axis=0)
        b_tail = b[CHUNK - SUBLANES:CHUNK, :]
        for p in range(n_pairs):
            ls = pair_lanes[p]
            dec_col[c, p] = jnp.exp(b_tail[:, ls].T[:, SUBLANES - 1:SUBLANES])
            k_e_bd[c, p] = jnp.concatenate([jnp.where(head0_lanes, k_e[:, ls], zk),
                                            jnp.where(head0_lanes, zk, k_e[:, ls])], axis=0)
            v_p = v[rs, pair_vlanes[p]]
            v_bd[c, p] = jnp.concatenate([jnp.where(head0_vlanes, v_p, zv),
                                          jnp.where(head0_vlanes, zv, v_p)], axis=0)
    yield "gla_decay"
    conv_piece("ch", 0)
    yield "conv"

    sc, kv = {}, {}
    for c in range(n_chunks):
        sc[c] = lax.dot_general(q_i[c], k_bd[c], _NT, preferred_element_type=f32)
        for p in range(n_pairs):
            kv[c, p] = lax.dot_general(k_e_bd[c, p], v_bd[c, p], _TN, preferred_element_type=f32)
    yield "gla_scores"
    conv_piece("cb", 0)
    conv_finish(0)
    yield "conv"

    st = [st_s[p] for p in range(n_pairs)]
    st_prev = {}
    for c in range(n_chunks):
        sc[c] = jnp.where(causal4, sc[c], 0.0).astype(bf16)
        for p in range(n_pairs):
            st_prev[c, p] = st[p].astype(bf16)
            st[p] = st[p] * dec_col[c, p] + kv[c, p]
    for p in range(n_pairs):
        st_s[p] = st[p]
    yield "gla_state"
    conv_piece("cc", 1)
    yield "conv"

    o = {}
    for c in range(n_chunks):
        for p in range(n_pairs):
            ls = pair_lanes[p]
            lhs = jnp.concatenate([sc[c][:, ls], q_i[c][:, ls]], axis=1)
            rhs = jnp.concatenate([v_bd[c, p], st_prev[c, p]], axis=0)
            o[c, p] = _dot(lhs, rhs)
    yield "gla_out"
    conv_piece("ch", 1)
    yield "conv"

    for c, rs in enumerate(chunk_rows):
        for hd in range(GLA_HEADS):
            hs = slice((hd % 2) * GLA_DV, (hd % 2 + 1) * GLA_DV)
            g_h = pg[rs, hd * GLA_DV:(hd + 1) * GLA_DV]
            mix_s[rs, hd * GLA_DV:(hd + 1) * GLA_DV] = (
                _rms(o[c, hd // 2][:, hs], gn) * (g_h * _sigmoid(g_h))).astype(bf16)
    yield "gla_gate"
    conv_piece("cb", 1)
    conv_finish(1)
    yield "conv"

    sum_sq = jnp.zeros((rows, 1), f32)
    for n in range(d_model // MXU_N):
        cs = slice(n * MXU_N, (n + 1) * MXU_N)
        x1 = x_ref[:, cs] + _dot(mix_s[...], wout_ref[:, cs])
        x1_s[:, cs] = x1
        sum_sq = sum_sq + jnp.sum(x1 * x1, axis=-1, keepdims=True)
        yield "out_proj"
    h2_s[...] = (x1_s[...] * _rms_scale(sum_sq, d_model) * g2_ref[...]).astype(bf16)
    yield "norm2"


def _ffn_items(wg_ref, wu_ref, wd_ref, gf_ref, out_ref, x1_s, xf_s, h2_s, a_s,
               *, rows, d_model, d_ff, final_norm):
    f32, bf16 = jnp.float32, jnp.bfloat16
    for j in range(d_ff // MXU_N):
        if j == 2:
            xf_s[...] = x1_s[...]
            yield "copy"
        cs = slice(j * MXU_N, (j + 1) * MXU_N)
        h2 = h2_s[...]
        g = _dot(h2, wg_ref[:, cs])
        u = _dot(h2, wu_ref[:, cs])
        a_s[:, cs] = ((g * _sigmoid(g)) * u).astype(bf16)
        yield "gate_up"
    sum_sq = jnp.zeros((rows, 1), f32)
    for n in range(d_model // MXU_N):
        cs = slice(n * MXU_N, (n + 1) * MXU_N)
        y = xf_s[:, cs] + _dot(a_s[...], wd_ref[:, cs])
        out_ref[:, cs] = y
        sum_sq = sum_sq + jnp.sum(y * y, axis=-1, keepdims=True)
        yield "down"
    if final_norm:
        out_ref[...] = out_ref[...] * _rms_scale(sum_sq, d_model) * gf_ref[...]
    yield "final_norm"


_ISSUE_ORDER = (
    "f m "
    "f m m "
    "f f m m "
    "f m m "
    "f m m "
    "f m m "
    "f m m "
    "f m m "
    "f m m "
    "f m "
    "m f m "
    "f m f m "
    "m f f f"
).split()


def _prep_rows(shape):
    n_rows, width = shape
    rows = SUBLANES
    while 2 * rows * width * 4 <= PREP_DMA_BYTES and n_rows % (2 * rows) == 0:
        rows *= 2
    return rows


def _prepare_w_in(w_in_t_hbm, w_s):
    n_cols, d_model = w_in_t_hbm.shape
    a0 = 2 * QK_W + 2 * V_W
    a1 = a0 + GATE_RANK
    assert a0 % LANES == 0 and (n_cols - a1) % LANES == 0 and a0 + LANES <= n_cols
    blocks = [(r, r, LANES) for r in range(0, a0, LANES)]
    blocks += [(a1 + r, a0 + r, LANES) for r in range(0, n_cols - a1, LANES)]
    blocks += [(a0, a0 + n_cols - a1, GATE_RANK)]
    ahead = PREP_SLOTS_T - 1
    lane = lax.broadcasted_iota(jnp.int32, (d_model, LANES), 1)

    def body(stage, sem):
        def copy(j):
            slot = j % PREP_SLOTS_T
            return pltpu.make_async_copy(
                w_in_t_hbm.at[pl.ds(blocks[j][0], LANES), :], stage.at[slot], sem.at[slot])

        for j in range(min(ahead, len(blocks))):
            copy(j).start()
        for j, (_, dst, keep) in enumerate(blocks):
            copy(j).wait()
            if j + ahead < len(blocks):
                copy(j + ahead).start()
            blk = stage[j % PREP_SLOTS_T].T
            if keep < LANES:
                blk = jnp.where(lane < keep, blk, 0.0)
            w_s[:, dst:dst + LANES] = blk.astype(jnp.bfloat16)

    pl.run_scoped(body, pltpu.VMEM((PREP_SLOTS_T, LANES, d_model), jnp.float32),
                  pltpu.SemaphoreType.DMA((PREP_SLOTS_T,)))


def _prepare_gate_up(wup_ref, wup_s):
    wup_s[...] = jnp.zeros_like(wup_s)
    wup_s[0:GATE_RANK, :] = wup_ref[...].astype(jnp.bfloat16)


def _weight_cast_items(weights, rings):
    ahead = PREP_SLOTS - 1
    by_ring = {}
    for (src, dst), ring in zip(weights, rings):
        rows = _prep_rows(src.shape)
        assert src.shape[0] % rows == 0
        by_ring.setdefault(id(ring[0]), (ring, []))[1].extend(
            (src, dst, r, rows) for r in range(0, src.shape[0], rows))

    def copy(ring, blocks, j):
        stage, sem = ring
        src, _, r, rows = blocks[j]
        slot = j % PREP_SLOTS
        return pltpu.make_async_copy(src.at[pl.ds(r, rows), :], stage.at[slot], sem.at[slot])

    for ring, blocks in by_ring.values():
        for j in range(min(ahead, len(blocks))):
            copy(ring, blocks, j).start()
    for ring, blocks in by_ring.values():
        for j, (_, dst, r, rows) in enumerate(blocks):
            copy(ring, blocks, j).wait()
            if j + ahead < len(blocks):
                copy(ring, blocks, j + ahead).start()
            dst[r:r + rows, :] = ring[0][j % PREP_SLOTS].astype(jnp.bfloat16)
            if j % CAST_BLOCKS_PER_ITEM == CAST_BLOCKS_PER_ITEM - 1 or j == len(blocks) - 1:
                yield "cast"


def _block_kernel(x_ref, g1_ref, w_in_t_hbm, wup_ref, bg_ref, gn_ref, cw_ref, wout_hbm,
                  g2_ref, wg_hbm, wu_hbm, wd_hbm, gf_ref, out_ref,
                  u_s, st_s, mix_s, x1_s, xf_s, h2_s, a_s,
                  w_s, wup_s, wout_s, wg_s, wu_s, wd_s,
                  *, rows, conv_w, d_model, d_ff, tiles_per_seq, final_norm):
    t = pl.program_id(0)

    @pl.when(t % tiles_per_seq == 0)
    def _():
        st_s[...] = jnp.zeros_like(st_s)
        u_s[0:SUBLANES, :] = jnp.zeros((SUBLANES, conv_w), jnp.float32)

    def mixer_items():
        return _mixer_items(x_ref, g1_ref, w_s, wup_s, bg_ref, gn_ref, cw_ref, wout_s,
                            g2_ref, u_s, st_s, mix_s, x1_s, h2_s,
                            rows=rows, conv_w=conv_w, d_model=d_model)

    @pl.when(t == 0)
    def _():
        _prepare_w_in(w_in_t_hbm, w_s)
        _prepare_gate_up(wup_ref, wup_s)

        def body(stage_gu, sem_gu, stage_d, sem_d):
            ring_gu, ring_d = (stage_gu, sem_gu), (stage_d, sem_d)
            casts = _weight_cast_items(
                ((wout_hbm, wout_s), (wd_hbm, wd_s), (wg_hbm, wg_s), (wu_hbm, wu_s)),
                (ring_d, ring_d, ring_gu, ring_gu))
            mixer = mixer_items()
            for i, _ in enumerate(mixer):
                if i % 2 == 1:
                    next(casts, None)
            for _ in casts:
                pass

        pl.run_scoped(
            body,
            pltpu.VMEM((PREP_SLOTS, _prep_rows(wg_hbm.shape)) + wg_hbm.shape[1:], jnp.float32),
            pltpu.SemaphoreType.DMA((PREP_SLOTS,)),
            pltpu.VMEM((PREP_SLOTS, _prep_rows(wd_hbm.shape)) + wd_hbm.shape[1:], jnp.float32),
            pltpu.SemaphoreType.DMA((PREP_SLOTS,)))

    @pl.when(t > 0)
    def _():
        streams = {
            "m": mixer_items(),
            "f": _ffn_items(wg_s, wu_s, wd_s, gf_ref, out_ref, x1_s, xf_s, h2_s, a_s,
                            rows=rows, d_model=d_model, d_ff=d_ff, final_norm=final_norm),
        }
        for which in _ISSUE_ORDER:
            next(streams[which])
        for stream in streams.values():
            assert next(stream, None) is None


def _resident(shape):
    return pl.BlockSpec(shape, lambda *_: (0,) * len(shape), pipeline_mode=pl.Buffered(1))


def _block(x2d, seq, g1, w_in_t, w_up, b_gate, gn, cw, w_out, g2, wg, wu, wd, gf, final_norm):
    n_rows, d = x2d.shape
    rows = TILE_ROWS
    conv_w = cw.shape[1] // CONV_K
    d_ff = wg.shape[1]
    n_in = w_in_t.shape[0] - GATE_RANK + LANES
    assert seq % rows == 0 and rows % CHUNK == 0 and n_rows % seq == 0
    assert V_W + conv_w == w_out.shape[0] and conv_w % (2 * MXU_N) == 0
    assert w_in_t.shape[0] == 2 * QK_W + 2 * V_W + GATE_RANK + 3 * conv_w
    assert d % MXU_N == 0 and d_ff % MXU_N == 0 and GATE_RANK <= LANES and wg.shape == wu.shape
    assert w_out.shape[1] == wd.shape[1] and _prep_rows(w_out.shape) == _prep_rows(wd.shape)
    n_tiles = n_rows // rows
    f32, bf16 = jnp.float32, jnp.bfloat16
    kern = functools.partial(_block_kernel, rows=rows, conv_w=conv_w, d_model=d, d_ff=d_ff,
                             tiles_per_seq=seq // rows, final_norm=final_norm)
    operands = (g1, w_in_t, w_up, b_gate, gn, cw, w_out, g2, wg, wu, wd, gf)
    in_hbm = {id(w) for w in (w_in_t, w_out, wg, wu, wd)}
    specs = [pl.BlockSpec(memory_space=pl.ANY) if id(w) in in_hbm else _resident(w.shape)
             for w in operands]
    scratch = [
        ((rows + SUBLANES, conv_w), f32),
        ((GLA_HEADS // 2, LANES, 2 * GLA_DV), f32),
        ((rows, V_W + conv_w), bf16),
        ((rows, d), f32),
        ((rows, d), f32),
        ((rows, d), bf16),
        ((rows, d_ff), bf16),
        ((d, n_in), bf16),
        ((LANES, QK_W), bf16),
        (w_out.shape, bf16),
        (wg.shape, bf16),
        (wu.shape, bf16),
        (wd.shape, bf16),
    ]

    def nbytes(shape, dtype):
        return math.prod(shape) * jnp.dtype(dtype).itemsize

    def ring_bytes(shape, slots):
        return slots * nbytes((_prep_rows(shape),) + tuple(shape[1:]), f32)

    staging = max(PREP_SLOTS_T * nbytes((LANES, d), f32),
                  ring_bytes(wg.shape, PREP_SLOTS) + ring_bytes(wd.shape, PREP_SLOTS))
    vmem_bytes = (sum(nbytes(*sd) for sd in scratch) + 2 * 2 * nbytes((rows, d), f32)
                  + staging + LIVE_TILE_VALUES * nbytes((rows, d), f32))
    assert vmem_bytes <= V7X_VMEM_BYTES
    return pl.pallas_call(
        kern,
        grid=(n_tiles + 1,),
        in_specs=[pl.BlockSpec((rows, d), lambda t: (jnp.minimum(t, n_tiles - 1), 0))] + specs,
        out_specs=pl.BlockSpec((rows, d), lambda t: (jnp.maximum(t - 1, 0), 0)),
        out_shape=jax.ShapeDtypeStruct(x2d.shape, x2d.dtype),
        scratch_shapes=[pltpu.VMEM(shape, dtype) for shape, dtype in scratch],
        compiler_params=pltpu.CompilerParams(
            dimension_semantics=("arbitrary",),
            vmem_limit_bytes=vmem_bytes),
        name="block",
    )(x2d, *operands)


def kernel(x, norm1_g, w_in, w_gate_up, b_gate, gla_norm_g, conv_w, w_out, norm2_g,
           w_ffn_gate, w_ffn_up, w_ffn_down, norm_f_g):
    bsz, seq, d = x.shape
    depth = w_in.shape[0]
    assert depth >= 1 and conv_w.shape[1] == CONV_K
    x = x.reshape(bsz * seq, d)
    for l in range(depth):
        w_in_t = jnp.swapaxes(w_in[l], 0, 1)
        x = _block(x, seq, norm1_g[l][None, :], w_in_t, w_gate_up[l], b_gate[l][None, :],
                   gla_norm_g[l][None, :], conv_w[l].reshape(1, -1), w_out[l], norm2_g[l][None, :],
                   w_ffn_gate[l], w_ffn_up[l], w_ffn_down[l], norm_f_g[None, :],
                   final_norm=(l == depth - 1))
    return x.reshape(bsz, seq, d)
```
